```python
import math
import jax
import jax.numpy as jnp
from jax import lax
import numpy as np

D_MODEL = 1024
BATCH = 1
SEQ = 16384
DEPTH = 1
DEC_BATCH = 128
DEC_SEQ = 1
PAST_LEN = 8192
PAGE_SIZE = 128

DA_HEADS = 8
DA_QK_DIM = 32
DA_V_DIM = 64
ML_HEADS = 8
ML_HEAD_DIM = 64
ML_CHUNK = 64
CONV_W = 4
NUM_BUCKETS = 32
MAX_DISTANCE = 128
Q_BLOCK = 128
N_EXPERTS = 256
TOP_K = 8
N_GROUPS = 8
TOPK_GROUPS = 4
D_EXPERT = 256
ROUTED_SCALE = 2.5
MOE_BLOCK = 64
EPS = 1e-6

DA_QK_COLS = DA_HEADS * 2 * DA_QK_DIM
DA_V_COLS = DA_HEADS * DA_V_DIM
ML_COLS = ML_HEADS * ML_HEAD_DIM
IN_SECTIONS = (DA_QK_COLS, DA_QK_COLS, DA_V_COLS, 2 * ML_COLS, ML_COLS, ML_COLS, ML_HEADS, ML_HEADS)
D_IN = sum(IN_SECTIONS)
D_MIX = DA_V_COLS + ML_COLS

kernel_name = 'hymba_diffattn_mlstm_moe_step'


def rmsnorm(x, g):
    xf = x.astype(jnp.float32)
    y = xf * lax.rsqrt(jnp.mean(xf * xf, axis=-1, keepdims=True) + EPS)
    return (y * g.astype(jnp.float32)).astype(x.dtype)


def adaln(c, w, b):
    mod = jax.nn.silu(c) @ w + b
    return jnp.split(mod[:, None, :], 6, axis=-1)


def modulate(x, g, shift, scale):
    return rmsnorm(x, g) * (1 + scale) + shift


def t5_bucket(dist):
    n = jnp.maximum(dist, 0)
    max_exact = NUM_BUCKETS // 2
    nf = jnp.maximum(n, 1).astype(jnp.float32)
    large = max_exact + (jnp.log(nf / max_exact) / math.log(MAX_DISTANCE / max_exact)
                         * (NUM_BUCKETS - max_exact)).astype(jnp.int32)
    return jnp.where(n < max_exact, n, jnp.minimum(large, NUM_BUCKETS - 1))


def diff_attn_core(q, k, v, q_pos, k_pos, lam, rel_bias):
    logits = jnp.einsum('bqhmd,bkhmd->bhmqk', q, k).astype(jnp.float32) * DA_QK_DIM ** -0.5
    bias = jnp.moveaxis(rel_bias.astype(jnp.float32)[t5_bucket(q_pos[:, None] - k_pos[None, :])], -1, 0)
    causal = k_pos[None, :] <= q_pos[:, None]
    logits = jnp.where(causal, logits + bias[None, :, None], -jnp.inf)
    p = jax.nn.softmax(logits, axis=-1)
    attn = p[:, :, 0] - lam * p[:, :, 1]
    return jnp.einsum('bhqk,bkhd->bqhd', attn.astype(v.dtype), v)


def causal_conv(xc, buf, w, b):
    T = xc.shape[1]
    xp = jnp.concatenate([buf.astype(xc.dtype), xc], axis=1)
    y = b + xp[:, 0:T] * w[0]
    for j in range(1, CONV_W):
        y = y + xp[:, j:j + T] * w[j]
    return jax.nn.silu(y), xp[:, T:]


def mixer_inputs(h, conv_buf, w_in, conv_w, conv_b, b_i, b_f):
    B, T, _ = h.shape
    split_at = np.cumsum(IN_SECTIONS)[:-1].tolist()
    da_q, da_k, da_v, ml_qk, ml_v, ml_o, ml_i, ml_f = jnp.split(h @ w_in, split_at, axis=-1)
    qk, new_buf = causal_conv(ml_qk, conv_buf, conv_w, conv_b)
    mq, mk = jnp.split(qk, 2, axis=-1)

    def heads(a):
        return jnp.swapaxes(a.reshape(B, T, ML_HEADS, ML_HEAD_DIM), 1, 2).astype(jnp.float32)

    mlin = (heads(mq), heads(mk) * ML_HEAD_DIM ** -0.5, heads(ml_v),
            jnp.swapaxes((ml_i + b_i).astype(jnp.float32), 1, 2),
            jax.nn.log_sigmoid(jnp.swapaxes((ml_f + b_f).astype(jnp.float32), 1, 2)))
    og = jax.nn.sigmoid(ml_o).reshape(B, T, ML_HEADS, ML_HEAD_DIM)
    qa = da_q.reshape(B, T, DA_HEADS, 2, DA_QK_DIM)
    ka = da_k.reshape(B, T, DA_HEADS, 2, DA_QK_DIM)
    va = da_v.reshape(B, T, DA_HEADS, DA_V_DIM)
    return qa, ka, va, mlin, og, new_buf


def mlstm_chunk(carry, inp):
    C, n, m = carry
    q, k, v, ig, lf = inp
    L = q.shape[2]
    F = jnp.cumsum(lf, axis=-1)
    causal = jnp.tril(jnp.ones((L, L), dtype=bool))
    dlog = jnp.where(causal, F[..., :, None] - F[..., None, :] + ig[..., None, :], -jnp.inf)
    inter = m[..., None] + F
    m_t = jnp.maximum(inter, jnp.max(dlog, axis=-1))
    s = jnp.einsum('bhtd,bhsd->bhts', q, k) * jnp.exp(dlog - m_t[..., None])
    inter_w = jnp.exp(inter - m_t)
    num = jnp.einsum('bhts,bhsd->bhtd', s, v) + inter_w[..., None] * jnp.einsum('bhtd,bhde->bhte', q, C)
    den = jnp.sum(s, axis=-1) + inter_w * jnp.einsum('bhtd,bhd->bht', q, n)
    h = num / jnp.maximum(jnp.abs(den), jnp.exp(-m_t))[..., None]
    m_new = m_t[..., -1]
    decay = jnp.exp(F[..., -1:] - F + ig - m_new[..., None])
    carry_w = jnp.exp(m + F[..., -1] - m_new)
    C_new = carry_w[..., None, None] * C + jnp.einsum('bhs,bhsd,bhse->bhde', decay, k, v)
    n_new = carry_w[..., None] * n + jnp.einsum('bhs,bhsd->bhd', decay, k)
    return (C_new, n_new, m_new), h


def mlstm_scan(state, mlin):
    B, H, T, Dh = mlin[0].shape
    nc = T // ML_CHUNK

    def chunks(a):
        return jnp.moveaxis(a.reshape(a.shape[:2] + (nc, ML_CHUNK) + a.shape[3:]), 2, 0)

    state, h = lax.scan(mlstm_chunk, state, tuple(chunks(a) for a in mlin))
    return state, jnp.moveaxis(h, 0, 2).reshape(B, H, T, Dh)


def mixer_output(attn, hm, og, lam_init, da_g, ml_g, w_out):
    B, T = og.shape[:2]
    a = (rmsnorm(attn, da_g) * (1 - lam_init)).astype(og.dtype)
    mo = rmsnorm(jnp.swapaxes(hm, 1, 2), ml_g).astype(og.dtype) * og
    cat = jnp.concatenate([a.reshape(B, T, DA_V_COLS), mo.reshape(B, T, ML_COLS)], axis=-1)
    return cat @ w_out


def swiglu(x, wg, wu, wd):
    return (jax.nn.silu(x @ wg) * (x @ wu)) @ wd


def moe_ffn(h, w_router, router_bias, w_eg, w_eu, w_ed, w_sg, w_su, w_sd):
    T, D = h.shape
    aff = jax.nn.sigmoid((h @ w_router).astype(jnp.float32))
    sel = aff + router_bias.astype(jnp.float32)
    grp_score = lax.top_k(sel.reshape(T, N_GROUPS, N_EXPERTS // N_GROUPS), 2)[0].sum(-1)
    _, gidx = lax.top_k(grp_score, TOPK_GROUPS)
    gmask = jax.nn.one_hot(gidx, N_GROUPS, dtype=jnp.float32).sum(1) > 0
    sel = jnp.where(jnp.repeat(gmask, N_EXPERTS // N_GROUPS, axis=1), sel, -jnp.inf)
    _, eidx = lax.top_k(sel, TOP_K)
    wts = jnp.take_along_axis(aff, eidx, axis=1)
    wts = wts / jnp.sum(wts, axis=-1, keepdims=True) * ROUTED_SCALE
    tk = T * TOP_K
    flat_e = eidx.reshape(-1)
    order = jnp.argsort(flat_e)
    sorted_e = flat_e[order]
    counts = jnp.bincount(flat_e, length=N_EXPERTS)
    padded = (counts + MOE_BLOCK - 1) // MOE_BLOCK * MOE_BLOCK
    pad_end = jnp.cumsum(padded)
    dest = (pad_end - padded)[sorted_e] + jnp.arange(tk) - (jnp.cumsum(counts) - counts)[sorted_e]
    n_blocks = -(-tk // MOE_BLOCK) + N_EXPERTS
    rows = n_blocks * MOE_BLOCK
    row_tok = jnp.zeros((rows,), jnp.int32).at[dest].set((order // TOP_K).astype(jnp.int32))
    row_w = jnp.zeros((rows,), jnp.float32).at[dest].set(wts.reshape(-1)[order])
    block_e = jnp.minimum(jnp.searchsorted(pad_end, jnp.arange(n_blocks) * MOE_BLOCK, side='right'),
                          N_EXPERTS - 1)
    xb = h[row_tok].reshape(n_blocks, MOE_BLOCK, D)
    yb = lax.map(lambda a: swiglu(a[0], w_eg[a[1]], w_eu[a[1]], w_ed[a[1]]), (xb, block_e))
    routed = jnp.zeros((T, D), jnp.float32).at[row_tok].add(
        yb.reshape(rows, D).astype(jnp.float32) * row_w[:, None])
    return (routed + swiglu(h, w_sg, w_su, w_sd).astype(jnp.float32)).astype(h.dtype)


def setup_inputs(seed: int = 0) -> dict:
    key = jax.random.key(seed)
    ks = iter(jax.random.split(key, 48))
    f32 = jnp.float32

    def nrm(shape, scale):
        return jax.random.normal(next(ks), shape, f32) * scale

    n_pages = PAST_LEN // PAGE_SIZE
    n_pool = (DEC_BATCH * n_pages * 5) // 4
    page_table = jax.random.permutation(next(ks), n_pool)[: DEC_BATCH * n_pages].reshape(
        DEC_BATCH, n_pages).astype(jnp.int32)
    L = DEPTH
    return {
        'x_prompt': nrm((BATCH, SEQ, D_MODEL), 1.0),
        'x_sample': nrm((DEC_BATCH, DEC_SEQ, D_MODEL), 1.0),
        'cache_k': nrm((L, n_pool, PAGE_SIZE, DA_HEADS, 2 * DA_QK_DIM), 1.0),
        'cache_v': nrm((L, n_pool, PAGE_SIZE, DA_HEADS, DA_V_DIM), 1.0),
        'state_C': nrm((L, DEC_BATCH, ML_HEADS, ML_HEAD_DIM, ML_HEAD_DIM), 0.3),
        'state_n': nrm((L, DEC_BATCH, ML_HEADS, ML_HEAD_DIM), 0.5),
        'state_m': nrm((L, DEC_BATCH, ML_HEADS), 1.0),
        'state_conv': nrm((L, DEC_BATCH, CONV_W - 1, 2 * ML_COLS), 1.0),
        'page_table': page_table,
        'c_prompt': nrm((BATCH, D_MODEL), 1.0),
        'c_sample': nrm((DEC_BATCH, D_MODEL), 1.0),
        'w_ada': nrm((L, D_MODEL, 6 * D_MODEL), 0.5 * D_MODEL ** -0.5),
        'b_ada': nrm((L, 6 * D_MODEL), 0.1),
        'norm1_g': 1.0 + nrm((L, D_MODEL), 0.05),
        'norm2_g': 1.0 + nrm((L, D_MODEL), 0.05),
        'w_in': nrm((L, D_MODEL, D_IN), D_MODEL ** -0.5),
        'conv_w': nrm((L, CONV_W, 2 * ML_COLS), CONV_W ** -0.5),
        'conv_b': nrm((L, 2 * ML_COLS), 0.02),
        'lambda_q1': nrm((L, DA_QK_DIM), 0.1),
        'lambda_k1': nrm((L, DA_QK_DIM), 0.1),
        'lambda_q2': nrm((L, DA_QK_DIM), 0.1),
        'lambda_k2': nrm((L, DA_QK_DIM), 0.1),
        'da_norm_g': 1.0 + nrm((L, DA_HEADS, DA_V_DIM), 0.05),
        'rel_bias': nrm((NUM_BUCKETS, DA_HEADS), 0.5),
        'ml_b_i': nrm((L, ML_HEADS), 0.5),
        'ml_b_f': 3.0 + nrm((L, ML_HEADS), 0.5),
        'ml_norm_g': 1.0 + nrm((L, ML_HEADS, ML_HEAD_DIM), 0.05),
        'w_out': nrm((L, D_MIX, D_MODEL), D_MIX ** -0.5),
        'w_router': nrm((L, D_MODEL, N_EXPERTS), D_MODEL ** -0.5),
        'router_bias': nrm((L, N_EXPERTS), 0.01),
        'w_exp_gate': nrm((L, N_EXPERTS, D_MODEL, D_EXPERT), D_MODEL ** -0.5),
        'w_exp_up': nrm((L, N_EXPERTS, D_MODEL, D_EXPERT), D_MODEL ** -0.5),
        'w_exp_down': nrm((L, N_EXPERTS, D_EXPERT, D_MODEL), D_EXPERT ** -0.5),
        'w_sh_gate': nrm((L, D_MODEL, D_EXPERT), D_MODEL ** -0.5),
        'w_sh_up': nrm((L, D_MODEL, D_EXPERT), D_MODEL ** -0.5),
        'w_sh_down': nrm((L, D_EXPERT, D_MODEL), D_EXPERT ** -0.5),
        'final_norm_g': 1.0 + nrm((D_MODEL,), 0.05),
    }


def reference(x_prompt, x_sample, cache_k, cache_v, state_C, state_n, state_m, state_conv,
              page_table, c_prompt, c_sample, w_ada, b_ada, norm1_g, norm2_g, w_in, conv_w, conv_b,
              lambda_q1, lambda_k1, lambda_q2, lambda_k2, da_norm_g, rel_bias, ml_b_i, ml_b_f,
              ml_norm_g, w_out, w_router, router_bias, w_exp_gate, w_exp_up, w_exp_down,
              w_sh_gate, w_sh_up, w_sh_down, final_norm_g):
    f32 = jnp.float32
    B, T, D = x_prompt.shape
    Bs, Ts, _ = x_sample.shape
    past_len = page_table.shape[1] * PAGE_SIZE
    pos_p = jnp.arange(T)
    q_pos_blocks = pos_p.reshape(T // Q_BLOCK, Q_BLOCK)
    q_pos_s = past_len + jnp.arange(Ts)
    k_pos_s = jnp.arange(past_len + Ts)
    xp, xs = x_prompt, x_sample
    kp_l, vp_l, Cp_l, np_l, mp_l, cp_l = [], [], [], [], [], []
    ks_l, vs_l, Cs_l, ns_l, ms_l, cs_l = [], [], [], [], [], []
    for l in range(DEPTH):
        lam_init = 0.8 - 0.6 * math.exp(-0.3 * l)
        lam = (jnp.exp(jnp.sum(lambda_q1[l].astype(f32) * lambda_k1[l].astype(f32)))
               - jnp.exp(jnp.sum(lambda_q2[l].astype(f32) * lambda_k2[l].astype(f32))) + lam_init)
        sh1p, sc1p, g1p, sh2p, sc2p, g2p = adaln(c_prompt, w_ada[l], b_ada[l])
        sh1s, sc1s, g1s, sh2s, sc2s, g2s = adaln(c_sample, w_ada[l], b_ada[l])

        hp = modulate(xp, norm1_g[l], sh1p, sc1p)
        qa, ka, va, mlin, og, conv_new_p = mixer_inputs(
            hp, jnp.zeros((B, CONV_W - 1, 2 * ML_COLS), hp.dtype), w_in[l], conv_w[l], conv_b[l],
            ml_b_i[l], ml_b_f[l])
        q_blocks = jnp.moveaxis(qa.reshape(B, T // Q_BLOCK, Q_BLOCK, DA_HEADS, 2, DA_QK_DIM), 1, 0)
        attn_p = lax.map(lambda a: diff_attn_core(a[0], ka, va, a[1], pos_p, lam, rel_bias),
                         (q_blocks, q_pos_blocks))
        attn_p = jnp.moveaxis(attn_p, 0, 1).reshape(B, T, DA_HEADS, DA_V_DIM)
        zero_state = (jnp.zeros((B, ML_HEADS, ML_HEAD_DIM, ML_HEAD_DIM), f32),
                      jnp.zeros((B, ML_HEADS, ML_HEAD_DIM), f32), jnp.zeros((B, ML_HEADS), f32))
        (Cp, n_p, m_p), hm_p = mlstm_scan(zero_state, mlin)
        xp = xp + g1p * mixer_output(attn_p, hm_p, og, lam_init, da_norm_g[l], ml_norm_g[l], w_out[l])

        hs = modulate(xs, norm1_g[l], sh1s, sc1s)
        qs, kn, vn, mlin_s, og_s, conv_new_s = mixer_inputs(
            hs, state_conv[l], w_in[l], conv_w[l], conv_b[l], ml_b_i[l], ml_b_f[l])
        k_past = cache_k[l, page_table].reshape(Bs, past_len, DA_HEADS, 2, DA_QK_DIM)
        v_past = cache_v[l, page_table].reshape(Bs, past_len, DA_HEADS, DA_V_DIM)
        attn_s = diff_attn_core(qs, jnp.concatenate([k_past.astype(kn.dtype), kn], axis=1),
                                jnp.concatenate([v_past.astype(vn.dtype), vn], axis=1),
                                q_pos_s, k_pos_s, lam, rel_bias)
        carry_s = (state_C[l].astype(f32), state_n[l].astype(f32), state_m[l].astype(f32))
        (Cs, n_s, m_s), hm_s = mlstm_chunk(carry_s, mlin_s)
        xs = xs + g1s * mixer_output(attn_s, hm_s, og_s, lam_init, da_norm_g[l], ml_norm_g[l], w_out[l])

        hp2 = modulate(xp, norm2_g[l], sh2p, sc2p)
        hs2 = modulate(xs, norm2_g[l], sh2s, sc2s)
        ff = moe_ffn(jnp.concatenate([hp2.reshape(B * T, D), hs2.reshape(Bs * Ts, D)], axis=0),
                     w_router[l], router_bias[l], w_exp_gate[l], w_exp_up[l], w_exp_down[l],
                     w_sh_gate[l], w_sh_up[l], w_sh_down[l])
        xp = xp + g2p * ff[: B * T].reshape(B, T, D)
        xs = xs + g2s * ff[B * T:].reshape(Bs, Ts, D)

        kp_l.append(ka.reshape(B, T, DA_HEADS, 2 * DA_QK_DIM))
        vp_l.append(va)
        Cp_l.append(Cp.astype(state_C.dtype))
        np_l.append(n_p.astype(state_n.dtype))
        mp_l.append(m_p.astype(state_m.dtype))
        cp_l.append(conv_new_p.astype(state_conv.dtype))
        ks_l.append(kn.reshape(Bs, Ts, DA_HEADS, 2 * DA_QK_DIM))
        vs_l.append(vn)
        Cs_l.append(Cs.astype(state_C.dtype))
        ns_l.append(n_s.astype(state_n.dtype))
        ms_l.append(m_s.astype(state_m.dtype))
        cs_l.append(conv_new_s.astype(state_conv.dtype))

    y_prompt = rmsnorm(xp, final_norm_g)
    y_sample = rmsnorm(xs, final_norm_g)
    return (y_prompt, y_sample,
            jnp.stack(kp_l), jnp.stack(vp_l), jnp.stack(Cp_l), jnp.stack(np_l), jnp.stack(mp_l), jnp.stack(cp_l),
            jnp.stack(ks_l), jnp.stack(vs_l), jnp.stack(Cs_l), jnp.stack(ns_l), jnp.stack(ms_l), jnp.stack(cs_l))
```

```python
import functools
import math

import jax
import jax.numpy as jnp
import numpy as np
from jax import lax
from jax.experimental import pallas as pl
from jax.experimental.pallas import tpu as pltpu

F32 = jnp.float32
BF16 = jnp.bfloat16
I32 = jnp.int32
HI = lax.Precision.HIGHEST

DA_HEADS = 8
DA_QK_DIM = 32
DA_V_DIM = 64
ML_HEADS = 8
ML_HEAD_DIM = 64
CONV_W = 4
NUM_BUCKETS = 32
MAX_DISTANCE = 128
PAGE_SIZE = 128
N_GROUPS = 8
TOPK_GROUPS = 4
TOP_K = 8
ROUTED_SCALE = 2.5
EPS = 1e-6

DA_COLS = DA_HEADS * 2 * DA_QK_DIM
ML_COLS = ML_HEADS * ML_HEAD_DIM
N_GATES = 2 * ML_HEADS
Q_SCALE = DA_QK_DIM ** -0.5
K_SCALE = ML_HEAD_DIM ** -0.5
NEG = -1e30

LANES = 128
SUBLANES = 8
VMEM_LIMIT = 56 * 1024 * 1024

ML_CHUNK = 128
ATT_TILE = 512
MOE_BM = 128
TOK_BLK = 128
PAGES_PER_STEP = 16
PROJ_TM = 256

NT = (((1,), (1,)), ((), ()))


def _const_spec(shape):
    nd = len(shape)
    return pl.BlockSpec(shape, lambda *_: (0,) * nd)


def _params(sem, vmem=VMEM_LIMIT):
    return pltpu.CompilerParams(dimension_semantics=sem, vmem_limit_bytes=vmem)


def _sigmoid(x):
    return 1.0 / (1.0 + jnp.exp(-x))


def _log_sigmoid(x):
    return jnp.minimum(x, 0.0) - jnp.log1p(jnp.exp(-jnp.abs(x)))


def _lam(lq1, lk1, lq2, lk2, lam_init):
    a = jnp.sum(lq1[...] * lk1[...], axis=-1, keepdims=True)
    b = jnp.sum(lq2[...] * lk2[...], axis=-1, keepdims=True)
    return jnp.exp(a) - jnp.exp(b) + lam_init


def _modulated_norm(x, g, shift, scale):
    y = x * lax.rsqrt(jnp.mean(x * x, axis=-1, keepdims=True) + EPS) * g
    return y * (1.0 + scale) + shift


def _t5_bucket(dist):
    n = jnp.maximum(dist, 0)
    max_exact = NUM_BUCKETS // 2
    nf = jnp.maximum(n, 1).astype(F32)
    large = max_exact + (jnp.log(nf / max_exact) / math.log(MAX_DISTANCE / max_exact)
                         * (NUM_BUCKETS - max_exact)).astype(I32)
    return jnp.where(n < max_exact, n, jnp.minimum(large, NUM_BUCKETS - 1))


def _bias_by_distance(rel_bias, n):
    rb = rel_bias.astype(F32)
    return rb[_t5_bucket(jnp.arange(n))] - rb[NUM_BUCKETS - 1][None, :]


def _adaln_kernel(c_ref, w_ref, b_ref, o_ref):
    c = c_ref[...]
    a = c * _sigmoid(c)
    o_ref[...] = jnp.dot(a, w_ref[...], precision=HI, preferred_element_type=F32) + b_ref[...]


def _adaln(c_all, w, b):
    r, d = c_all.shape
    n = w.shape[1]
    tn = n // 4
    return pl.pallas_call(
        _adaln_kernel,
        grid=(n // tn,),
        in_specs=[_const_spec((r, d)), pl.BlockSpec((d, tn), lambda j: (0, j)),
                  pl.BlockSpec((1, tn), lambda j: (0, j))],
        out_specs=pl.BlockSpec((r, tn), lambda j: (0, j)),
        out_shape=jax.ShapeDtypeStruct((r, n), F32),
        compiler_params=_params(("parallel",)),
        name="adaln",
    )(c_all, w, b.reshape(1, n))


def _inproj_kernel(x_ref, sh_ref, sc_ref, g_ref, wq_ref, wkv_ref, wml_ref, wg_ref, bg_ref, bgt_ref,
                   qbf_ref, ktf_ref, ktb_ref, vtf_ref, vtb_ref, mqk_ref, mv_ref, mo_ref, gates_ref, gatest_ref):
    h = _modulated_norm(x_ref[...], g_ref[...], sh_ref[...], sc_ref[...])
    hb = h.astype(BF16)
    q = lax.dot_general(hb, wq_ref[...], NT, preferred_element_type=F32)
    qbf_ref[...] = (q * Q_SCALE).astype(BF16)
    kvt = lax.dot_general(wkv_ref[...], hb, NT, preferred_element_type=F32)
    ktf_ref[...] = kvt[0:DA_COLS, :]
    ktb_ref[...] = kvt[0:DA_COLS, :].astype(BF16)
    vtf_ref[...] = kvt[DA_COLS:2 * DA_COLS, :]
    vtb_ref[...] = kvt[DA_COLS:2 * DA_COLS, :].astype(BF16)
    ml = lax.dot_general(hb, wml_ref[...], NT, preferred_element_type=F32)
    mqk_ref[...] = ml[:, 0:2 * ML_COLS]
    mv_ref[...] = ml[:, 2 * ML_COLS:3 * ML_COLS].astype(BF16)
    mo_ref[...] = ml[:, 3 * ML_COLS:4 * ML_COLS]
    graw = lax.dot_general(h, wg_ref[...], NT, precision=HI, preferred_element_type=F32) + bg_ref[...]
    lane = lax.broadcasted_iota(I32, graw.shape, 1)
    gates_ref[...] = jnp.where(lane < ML_HEADS, graw, _log_sigmoid(graw))
    grawt = lax.dot_general(wg_ref[...], h, NT, precision=HI, preferred_element_type=F32) + bgt_ref[...]
    sub = lax.broadcasted_iota(I32, grawt.shape, 0)
    gatest_ref[...] = jnp.where(sub < ML_HEADS, grawt, _log_sigmoid(grawt))


def _inproj(x, shift, scale, g, wq_t, wkv_t, wml_t, wg_t, b_gate):
    r, d = x.shape
    tm = min(PROJ_TM, r)
    row = lambda n: pl.BlockSpec((tm, n), lambda i: (i, 0))
    col = lambda n: pl.BlockSpec((n, tm), lambda i: (0, i))
    ng = N_GATES
    out_shape = [
        jax.ShapeDtypeStruct((r, DA_COLS), BF16),
        jax.ShapeDtypeStruct((DA_COLS, r), F32), jax.ShapeDtypeStruct((DA_COLS, r), BF16),
        jax.ShapeDtypeStruct((DA_COLS, r), F32), jax.ShapeDtypeStruct((DA_COLS, r), BF16),
        jax.ShapeDtypeStruct((r, 2 * ML_COLS), F32), jax.ShapeDtypeStruct((r, ML_COLS), BF16),
        jax.ShapeDtypeStruct((r, ML_COLS), F32),
        jax.ShapeDtypeStruct((r, ng), F32), jax.ShapeDtypeStruct((ng, r), F32),
    ]
    out_specs = [row(DA_COLS), col(DA_COLS), col(DA_COLS), col(DA_COLS), col(DA_COLS), row(2 * ML_COLS),
                 row(ML_COLS), row(ML_COLS), row(ng), col(ng)]
    return pl.pallas_call(
        _inproj_kernel,
        grid=(r // tm,),
        in_specs=[row(d), _const_spec((1, d)), _const_spec((1, d)), _const_spec((1, d)),
                  _const_spec(wq_t.shape), _const_spec(wkv_t.shape), _const_spec(wml_t.shape),
                  _const_spec(wg_t.shape), _const_spec((1, ng)), _const_spec((ng, 1))],
        out_specs=out_specs,
        out_shape=out_shape,
        compiler_params=_params(("parallel",)),
        name="inproj",
    )(x, shift, scale, g, wq_t, wkv_t, wml_t, wg_t, b_gate.reshape(1, ng), b_gate.reshape(ng, 1))


def _pair_rmsnorm(x, lane, nd):
    sq = x * x
    lo = jnp.sum(jnp.where(lane < nd, sq, 0.0), axis=-1, keepdims=True) * (1.0 / nd)
    hi = jnp.sum(jnp.where(lane >= nd, sq, 0.0), axis=-1, keepdims=True) * (1.0 / nd)
    return x * jnp.where(lane < nd, lax.rsqrt(lo + EPS), lax.rsqrt(hi + EPS))


def _mlstm_kernel(mqk_ref, mv_ref, og_ref, gates_ref, gatest_ref, cw_ref, cb_ref, mlg_ref,
                  tri_ref, trit_ref, mo_ref, sout_ref, mout_ref, xprev, st, msc):
    c = pl.program_id(0)
    L = mqk_ref.shape[0]
    hd = ML_HEAD_DIM

    @pl.when(c == 0)
    def _():
        xprev[...] = jnp.zeros_like(xprev)
        st[...] = jnp.zeros_like(st)
        msc[...] = jnp.zeros_like(msc)

    x = mqk_ref[...]
    xp = xprev[...]
    row = lax.broadcasted_iota(I32, x.shape, 0)

    def shifted(j):
        return jnp.where(row < j, pltpu.roll(xp, j, 0), pltpu.roll(x, j, 0))

    y = cb_ref[...] + shifted(3) * cw_ref[0:1, :]
    y = y + shifted(2) * cw_ref[1:2, :]
    y = y + shifted(1) * cw_ref[2:3, :]
    y = y + x * cw_ref[3:4, :]
    qk = y * _sigmoid(y)
    xprev[...] = x

    g = gates_ref[...]
    gt = gatest_ref[...]
    fcol_all = jnp.dot(tri_ref[...], g, precision=HI, preferred_element_type=F32)
    frow_all = jnp.dot(gt, trit_ref[...], precision=HI, preferred_element_type=F32)
    r_i = lax.broadcasted_iota(I32, (L, L), 0)
    c_i = lax.broadcasted_iota(I32, (L, L), 1)
    causal = c_i <= r_i
    lane = lax.broadcasted_iota(I32, (L, 2 * hd), 1)

    for p in range(ML_HEADS // 2):
        qpair = qk[:, 2 * hd * p:2 * hd * (p + 1)]
        kpair = qk[:, ML_COLS + 2 * hd * p:ML_COLS + 2 * hd * (p + 1)] * K_SCALE
        vpair = mv_ref[:, 2 * hd * p:2 * hd * (p + 1)].astype(F32)
        hh, decays, vaugs, carries = [], [], [], []
        for j in range(2):
            h = 2 * p + j
            q_h = qpair[:, hd * j:hd * (j + 1)].astype(BF16)
            k_h = kpair[:, hd * j:hd * (j + 1)].astype(BF16)
            s_raw = lax.dot_general(q_h, k_h, NT, preferred_element_type=F32)
            fc = fcol_all[:, ML_HEADS + h:ML_HEADS + h + 1]
            fr = frow_all[ML_HEADS + h:ML_HEADS + h + 1, :]
            igr = gt[h:h + 1, :]
            igc = g[:, h:h + 1]
            m_old = msc[h]
            dlog = jnp.where(causal, fc - fr + igr, NEG)
            inter = m_old + fc
            m_t = jnp.maximum(inter, jnp.max(dlog, axis=-1, keepdims=True))
            dmat = jnp.where(causal, jnp.exp(dlog - m_t), 0.0)
            smat = (s_raw * dmat).astype(BF16)
            vsrc = vpair if j == 0 else pltpu.roll(vpair, hd, 1)
            vaug = jnp.where(lane < hd, vsrc, jnp.where(lane == hd, 1.0, 0.0)).astype(BF16)
            s_h = st[h]
            q_s = jnp.dot(q_h, s_h.astype(BF16), preferred_element_type=F32)
            num = jnp.dot(smat, vaug, preferred_element_type=F32) + jnp.exp(inter - m_t) * q_s
            den = num[:, hd:hd + 1]
            hh.append(num / jnp.maximum(jnp.abs(den), jnp.exp(-m_t)))
            m_new = m_t[L - 1:L, :]
            f_last = fc[L - 1:L, :]
            decays.append(jnp.exp(f_last - fc + igc - m_new))
            carries.append(jnp.exp(m_old + f_last - m_new))
            vaugs.append(vaug)
            msc[h] = m_new
        kd = kpair * jnp.where(lane < hd, decays[0], decays[1])
        kdt = kd.T
        for j in range(2):
            h = 2 * p + j
            upd = jnp.dot(kdt[hd * j:hd * (j + 1), :].astype(BF16), vaugs[j], preferred_element_type=F32)
            st[h] = carries[j] * st[h] + upd
        hpair = jnp.where(lane < hd, hh[0], pltpu.roll(hh[1], hd, 1))
        hn = _pair_rmsnorm(hpair, lane, hd) * mlg_ref[:, 2 * hd * p:2 * hd * (p + 1)]
        og = _sigmoid(og_ref[:, 2 * hd * p:2 * hd * (p + 1)])
        mo_ref[:, 2 * hd * p:2 * hd * (p + 1)] = (hn * og).astype(BF16)

    sout_ref[...] = st[...]
    mout_ref[...] = msc[...]


def _mlstm_prompt(mqk, mv, og, gates, gates_t, conv_w, conv_b, ml_g):
    t = mqk.shape[0]
    L = min(ML_CHUNK, t)
    hd = ML_HEAD_DIM
    tri = jnp.asarray(np.tril(np.ones((L, L), np.float32)))
    row = lambda n: pl.BlockSpec((L, n), lambda i: (i, 0))
    ng = N_GATES
    return pl.pallas_call(
        _mlstm_kernel,
        grid=(t // L,),
        in_specs=[row(2 * ML_COLS), row(ML_COLS), row(ML_COLS), row(ng),
                  pl.BlockSpec((ng, L), lambda i: (0, i)),
                  _const_spec((CONV_W, 2 * ML_COLS)), _const_spec((1, 2 * ML_COLS)),
                  _const_spec((1, ML_COLS)), _const_spec((L, L)), _const_spec((L, L))],
        out_specs=[row(ML_COLS), _const_spec((ML_HEADS, hd, 2 * hd)), _const_spec((ML_HEADS, 1, 1))],
        out_shape=[jax.ShapeDtypeStruct((t, ML_COLS), BF16),
                   jax.ShapeDtypeStruct((ML_HEADS, hd, 2 * hd), F32),
                   jax.ShapeDtypeStruct((ML_HEADS, 1, 1), F32)],
        scratch_shapes=[pltpu.VMEM((L, 2 * ML_COLS), F32), pltpu.VMEM((ML_HEADS, hd, 2 * hd), F32),
                        pltpu.VMEM((ML_HEADS, 1, 1), F32)],
        compiler_params=_params(("arbitrary",)),
        name="mlstm_prompt",
    )(mqk, mv, og, gates, gates_t, conv_w, conv_b.reshape(1, -1), ml_g.reshape(1, -1), tri, tri.T)


def _pattn_kernel(ii_ref, jj_ref, q_ref, kt_ref, vt_ref, bias_ref, dag_ref, lq1, lk1, lq2, lk2,
                  o_ref, m_sc, l_sc, acc_sc, *, lam_init):
    s_id = pl.program_id(1)
    i = ii_ref[s_id]
    j = jj_ref[s_id]
    dq = DA_QK_DIM
    dv = DA_V_DIM

    @pl.when(j == 0)
    def _():
        m_sc[...] = jnp.full_like(m_sc, NEG)
        l_sc[...] = jnp.zeros_like(l_sc)
        acc_sc[...] = jnp.zeros_like(acc_sc)

    def step(use_bias):
        q = q_ref[...]
        kt = kt_ref[...]
        vt = vt_ref[...]
        sub_v = lax.broadcasted_iota(I32, vt.shape, 0)
        vt_half = (jnp.where(sub_v < dv, vt, jnp.zeros_like(vt)), jnp.where(sub_v >= dv, vt, jnp.zeros_like(vt)))
        lane_a = lax.broadcasted_iota(I32, acc_sc.shape[1:], 1)
        for m in range(2):
            alphas, pv = [], None
            for hp in range(2):
                c0 = hp * 2 * dq + m * dq
                s = jnp.dot(q[:, c0:c0 + dq], kt[c0:c0 + dq, :], preferred_element_type=F32)
                if use_bias:
                    s = s + bias_ref[hp]
                idx = 2 * hp + m
                m_prev = m_sc[idx]
                m_new = jnp.maximum(m_prev, jnp.max(s, axis=-1, keepdims=True))
                p = jnp.exp(s - m_new)
                alpha = jnp.exp(m_prev - m_new)
                l_sc[idx] = alpha * l_sc[idx] + jnp.sum(p, axis=-1, keepdims=True)
                m_sc[idx] = m_new
                alphas.append(alpha)
                d = lax.dot_general(p.astype(BF16), vt_half[hp], NT, preferred_element_type=F32)
                pv = d if pv is None else pv + d
            acc_sc[m] = acc_sc[m] * jnp.where(lane_a < dv, alphas[0], alphas[1]) + pv

    near = j >= i - 1
    pl.when(near)(lambda: step(True))
    pl.when(jnp.logical_not(near))(lambda: step(False))

    @pl.when(j == i)
    def _():
        lane_a = lax.broadcasted_iota(I32, acc_sc.shape[1:], 1)
        lam = _lam(lq1, lk1, lq2, lk2, lam_init)
        inv1 = jnp.where(lane_a < dv, 1.0 / l_sc[0], 1.0 / l_sc[2])
        inv2 = jnp.where(lane_a < dv, 1.0 / l_sc[1], 1.0 / l_sc[3])
        o = acc_sc[0] * inv1 - lam * (acc_sc[1] * inv2)
        o = _pair_rmsnorm(o, lane_a, dv) * dag_ref[...]
        o_ref[...] = (o * (1.0 - lam_init)).astype(BF16)


def _prompt_bias_tiles(rel_bias, tile):
    tab_t = _bias_by_distance(rel_bias, 2 * tile).T
    r = jnp.arange(tile)[:, None]
    c = jnp.arange(tile)[None, :]
    d_diag = r - c
    diag = jnp.where((d_diag >= 0)[None], tab_t[:, jnp.maximum(d_diag, 0)], NEG)
    prev = tab_t[:, tile + r - c]
    return jnp.stack([prev, diag], axis=1).reshape(DA_HEADS // 2, 2, 2, tile, tile)


def _prompt_attention(q_bf, kt_bf, vt_bf, rel_bias, da_g, lams, lam_init):
    t = q_bf.shape[0]
    tile = min(ATT_TILE, t)
    assert tile > MAX_DISTANCE or tile == t
    nq = t // tile
    ii, jj = [], []
    for i in range(nq):
        for j in range(i + 1):
            ii.append(i)
            jj.append(j)
    ii = jnp.asarray(np.array(ii, np.int32))
    jj = jnp.asarray(np.array(jj, np.int32))
    bias = _prompt_bias_tiles(rel_bias, tile)
    w = 2 * DA_V_DIM
    lam_specs = [_const_spec((1, DA_QK_DIM))] * 4
    grid_spec = pltpu.PrefetchScalarGridSpec(
        num_scalar_prefetch=2,
        grid=(DA_HEADS // 2, ii.shape[0]),
        in_specs=[
            pl.BlockSpec((tile, w), lambda p, s, ii, jj: (ii[s], p)),
            pl.BlockSpec((w, tile), lambda p, s, ii, jj: (p, jj[s])),
            pl.BlockSpec((w, tile), lambda p, s, ii, jj: (p, jj[s])),
            pl.BlockSpec((None, 2, None, tile, tile),
                         lambda p, s, ii, jj: (p, 0, jnp.clip(jj[s] - ii[s] + 1, 0, 1), 0, 0)),
            pl.BlockSpec((1, w), lambda p, s, ii, jj: (0, p)),
        ] + lam_specs,
        out_specs=pl.BlockSpec((tile, w), lambda p, s, ii, jj: (ii[s], p)),
        scratch_shapes=[pltpu.VMEM((4, tile, 1), F32), pltpu.VMEM((4, tile, 1), F32),
                        pltpu.VMEM((2, tile, w), F32)],
    )
    return pl.pallas_call(
        functools.partial(_pattn_kernel, lam_init=lam_init),
        grid_spec=grid_spec,
        out_shape=jax.ShapeDtypeStruct((t, DA_COLS), BF16),
        compiler_params=_params(("parallel", "arbitrary")),
        name="prompt_attention",
    )(ii, jj, q_bf, kt_bf, vt_bf, bias, da_g.reshape(1, -1), *lams)


def _sinproj_kernel(x_ref, sh_ref, sc_ref, g_ref, wqkv_ref, wml_ref, wg_ref, bgt_ref,
                    qkvt_ref, mqk_ref, mvot_ref, gatest_ref):
    h = _modulated_norm(x_ref[...], g_ref[...], sh_ref[...], sc_ref[...])
    hb = h.astype(BF16)
    qkvt_ref[...] = lax.dot_general(wqkv_ref[...], hb, NT, preferred_element_type=F32)
    mqk_ref[...] = lax.dot_general(hb, wml_ref[0:2 * ML_COLS, :], NT, preferred_element_type=F32)
    mvot_ref[...] = lax.dot_general(wml_ref[2 * ML_COLS:4 * ML_COLS, :], hb, NT, preferred_element_type=F32)
    grawt = lax.dot_general(wg_ref[...], h, NT, precision=HI, preferred_element_type=F32) + bgt_ref[...]
    sub = lax.broadcasted_iota(I32, grawt.shape, 0)
    gatest_ref[...] = jnp.where(sub < ML_HEADS, grawt, _log_sigmoid(grawt))


def _sample_inproj(x, shift, scale, g, wqkv_t, wml_t, wg_t, b_gate):
    b, d = x.shape
    return pl.pallas_call(
        _sinproj_kernel,
        out_shape=[jax.ShapeDtypeStruct((3 * DA_COLS, b), F32), jax.ShapeDtypeStruct((b, 2 * ML_COLS), F32),
                   jax.ShapeDtypeStruct((2 * ML_COLS, b), F32), jax.ShapeDtypeStruct((N_GATES, b), F32)],
        compiler_params=pltpu.CompilerParams(vmem_limit_bytes=VMEM_LIMIT),
        name="sample_inproj",
    )(x, shift, scale, g, wqkv_t, wml_t, wg_t, b_gate.reshape(N_GATES, 1))


def _sconv_kernel(mqk_ref, buf_ref, cw_ref, cb_ref, qkt_ref, new_ref):
    x = mqk_ref[...]
    y = cb_ref[...] + buf_ref[0] * cw_ref[0:1, :]
    y = y + buf_ref[1] * cw_ref[1:2, :]
    y = y + buf_ref[2] * cw_ref[2:3, :]
    y = y + x * cw_ref[3:4, :]
    a = y * _sigmoid(y)
    lane = lax.broadcasted_iota(I32, a.shape, 1)
    qkt_ref[...] = jnp.where(lane < ML_COLS, a, a * K_SCALE).T
    new_ref[0] = buf_ref[1]
    new_ref[1] = buf_ref[2]
    new_ref[2] = x


def _sample_conv(mqk, conv_buf, conv_w, conv_b):
    b, n = mqk.shape
    return pl.pallas_call(
        _sconv_kernel,
        out_shape=[jax.ShapeDtypeStruct((n, b), F32), jax.ShapeDtypeStruct(conv_buf.shape, F32)],
        compiler_params=pltpu.CompilerParams(vmem_limit_bytes=VMEM_LIMIT),
        name="sample_conv",
    )(mqk, conv_buf, conv_w, conv_b.reshape(1, -1))


def _smlstm_kernel(q_ref, k_ref, v_ref, og_ref, ig_ref, lf_ref, m_ref, c_ref, n_ref, mlg_ref,
                   co_ref, no_ref, mo_ref, ho_ref):
    hd = ML_HEAD_DIM
    ig = ig_ref[...]
    lf = lf_ref[...]
    m_old = m_ref[...]
    m_t = jnp.maximum(m_old + lf, ig)
    a = jnp.exp(ig - m_t)
    w = jnp.exp(m_old + lf - m_t)
    v = v_ref[...]

    def body(d, num):
        cn = w * c_ref[d] + (a * k_ref[pl.ds(d, 1), :]) * v
        co_ref[d] = cn
        return num + q_ref[pl.ds(d, 1), :] * cn

    num = lax.fori_loop(0, hd, body, jnp.zeros(v.shape, F32))
    nn = w * n_ref[...] + a * k_ref[...]
    den = jnp.sum(q_ref[...] * nn, axis=0, keepdims=True)
    hv = num / jnp.maximum(jnp.abs(den), jnp.exp(-m_t))
    hn = hv * lax.rsqrt(jnp.mean(hv * hv, axis=0, keepdims=True) + EPS) * mlg_ref[...]
    no_ref[...] = nn
    mo_ref[...] = m_t
    ho_ref[...] = hn * _sigmoid(og_ref[...])


def _sample_mlstm(qk_t, v_t, og_t, gates_t, m_t, c_t, n_t, ml_g):
    b = qk_t.shape[1]
    h, d = ML_HEADS, ML_HEAD_DIM
    head_rows = lambda off: pl.BlockSpec((d, b), lambda i: (i + off, 0))
    vec = pl.BlockSpec((None, 1, b), lambda i: (i, 0, 0))
    return pl.pallas_call(
        _smlstm_kernel,
        grid=(h,),
        in_specs=[head_rows(0), head_rows(h), head_rows(0), head_rows(0),
                  vec, pl.BlockSpec((None, 1, b), lambda i: (i + h, 0, 0)), vec,
                  pl.BlockSpec((None, d, d, b), lambda i: (i, 0, 0, 0)),
                  pl.BlockSpec((None, d, b), lambda i: (i, 0, 0)),
                  pl.BlockSpec((None, d, 1), lambda i: (i, 0, 0))],
        out_specs=[pl.BlockSpec((None, d, d, b), lambda i: (i, 0, 0, 0)),
                   pl.BlockSpec((None, d, b), lambda i: (i, 0, 0)), vec, head_rows(0)],
        out_shape=[jax.ShapeDtypeStruct((h, d, d, b), F32), jax.ShapeDtypeStruct((h, d, b), F32),
                   jax.ShapeDtypeStruct((h, 1, b), F32), jax.ShapeDtypeStruct((h * d, b), F32)],
        compiler_params=_params(("parallel",)),
        name="sample_mlstm",
    )(qk_t, qk_t, v_t, og_t, gates_t.reshape(2 * h, 1, b), gates_t.reshape(2 * h, 1, b),
      m_t.reshape(h, 1, b), c_t, n_t, ml_g.reshape(h, d, 1))


def _sattn_kernel(pt_ref, qt_ref, knt_ref, vnt_ref, bpage_ref, bnew_ref, dag_ref, lq1, lk1, lq2, lk2,
                  ck_ref, cv_ref, o_ref, s_sc, a_sc, acc, qb, kbuf, vbuf, ksem, vsem, *, n_pg, lam_init):
    b = pl.program_id(0)
    ph = pl.program_id(1)
    g = pl.program_id(2)
    n_b = pl.num_programs(0)
    n_g = pl.num_programs(2)
    dq = DA_QK_DIM
    dv = DA_V_DIM
    nh = DA_HEADS
    n_pages = s_sc.shape[0] - 1
    lane = lax.broadcasted_iota(I32, qt_ref.shape, 1)

    def group_copies(cache_ref, buf, sem, bq, gq, slot):
        return [pltpu.make_async_copy(cache_ref.at[pt_ref[gq * n_pg + i, bq]], buf.at[slot, i], sem.at[slot])
                for i in range(n_pg)]

    def start_group(phq, bq, gq):
        slot = (bq * n_g + gq) % 2

        @pl.when(phq == 0)
        def _():
            for cp in group_copies(ck_ref, kbuf, ksem, bq, gq, slot):
                cp.start()

        @pl.when(phq == 1)
        def _():
            for cp in group_copies(cv_ref, vbuf, vsem, bq, gq, slot):
                cp.start()

    step = (b * 2 + ph) * n_g + g
    slot = (b * n_g + g) % 2

    @pl.when(step == 0)
    def _():
        start_group(ph, b, g)

    @pl.when(step + 1 < n_b * 2 * n_g)
    def _():
        nxt = step + 1
        start_group((nxt // n_g) % 2, nxt // (2 * n_g), nxt % n_g)

    k_refs = [kbuf.at[slot, i] for i in range(n_pg)]
    v_refs = [vbuf.at[slot, i] for i in range(n_pg)]

    def column(ref):
        return jnp.sum(jnp.where(lane == b, ref[...], 0.0), axis=1, keepdims=True)

    @pl.when((b == 0) & (ph == 0) & (g == 0))
    def _():
        o_ref[...] = jnp.zeros_like(o_ref)

    @pl.when((ph == 0) & (g == 0))
    def _():
        qb[...] = jnp.broadcast_to(column(qt_ref) * Q_SCALE, qb.shape)

    @pl.when(ph == 0)
    def _():
        for cp in group_copies(ck_ref, kbuf, ksem, b, g, slot):
            cp.wait()
        for c in range(2 * nh):
            qc = qb[c * dq:(c + 1) * dq, :]
            srow = (c % 2) * nh + c // 2
            for i in range(n_pg):
                prod = k_refs[i][c * dq:(c + 1) * dq, :] * qc
                s_sc[g * n_pg + i, srow:srow + 1, :] = jnp.sum(prod, axis=0, keepdims=True)

        @pl.when(g == n_g - 1)
        def _():
            s_sc[n_pages - 1] = s_sc[n_pages - 1] + bpage_ref[...]
            prodn = column(knt_ref) * qb[:, 0:1]
            sub = lax.broadcasted_iota(I32, (2 * nh, 1), 0)
            ln = bnew_ref[...]
            for r in range(2 * nh):
                c = (r % nh) * 2 + r // nh
                ln = ln + jnp.where(sub == r, jnp.sum(prodn[c * dq:(c + 1) * dq, :], axis=0, keepdims=True), 0.0)
            lane_s = lax.broadcasted_iota(I32, (2 * nh, LANES), 1)
            s_sc[n_pages] = jnp.where(lane_s == 0, ln, NEG)
            s_all = s_sc[...]
            mx = jnp.max(jnp.max(s_all, axis=0), axis=1, keepdims=True)
            p = jnp.exp(s_all - mx)
            l = jnp.sum(jnp.sum(p, axis=0), axis=1, keepdims=True)
            lam = _lam(lq1, lk1, lq2, lk2, lam_init)
            a_sc[...] = p[:, 0:nh, :] / l[0:nh, :] - lam * (p[:, nh:2 * nh, :] / l[nh:2 * nh, :])
            acc[...] = jnp.zeros_like(acc)

    @pl.when(ph == 1)
    def _():
        for cp in group_copies(cv_ref, vbuf, vsem, b, g, slot):
            cp.wait()
        for h in range(nh):
            part = acc[h * dv:(h + 1) * dv, :]
            for i in range(n_pg):
                part = part + v_refs[i][h * dv:(h + 1) * dv, :] * a_sc[g * n_pg + i, h:h + 1, :]
            acc[h * dv:(h + 1) * dv, :] = part

        @pl.when(g == n_g - 1)
        def _():
            out = jnp.sum(acc[...], axis=1, keepdims=True)
            vn = column(vnt_ref)
            a_new = a_sc[n_pages][:, 0:1]
            pieces = []
            for h in range(nh):
                oh = out[h * dv:(h + 1) * dv, :] + a_new[h:h + 1, :] * vn[h * dv:(h + 1) * dv, :]
                pieces.append(oh * lax.rsqrt(jnp.mean(oh * oh, axis=0, keepdims=True) + EPS))
            res = jnp.concatenate(pieces, axis=0) * dag_ref[...] * (1.0 - lam_init)
            o_ref[...] = jnp.where(lane == b, res, o_ref[...])


def _sample_attention(q_t, kn_t, vn_t, ck_t, cv_t, page_table_t, rel_bias, da_g, lams, lam_init):
    w, b = q_t.shape
    n_pages = page_table_t.shape[0]
    n_pg = min(PAGES_PER_STEP, n_pages)
    n_g = n_pages // n_pg
    ps = PAGE_SIZE
    past = n_pages * ps
    nh = DA_HEADS
    tab = _bias_by_distance(rel_bias, ps + 1)
    bpage = jnp.tile(tab[ps - jnp.arange(ps)].T, (2, 1))
    bnew = jnp.tile(tab[0:1].T, (2, 1))

    in_specs = [_const_spec((w, b))] * 3 + [_const_spec((2 * nh, ps)), _const_spec((2 * nh, 1)),
                                            _const_spec((w, 1))] + [_const_spec((1, DA_QK_DIM))] * 4
    in_specs += [pl.BlockSpec(memory_space=pl.ANY)] * 2
    grid_spec = pltpu.PrefetchScalarGridSpec(
        num_scalar_prefetch=1,
        grid=(b, 2, n_g),
        in_specs=in_specs,
        out_specs=_const_spec((w, b)),
        scratch_shapes=[pltpu.VMEM((n_pages + 1, 2 * nh, ps), F32), pltpu.VMEM((n_pages + 1, nh, ps), F32),
                        pltpu.VMEM((w, ps), F32), pltpu.VMEM((w, ps), F32),
                        pltpu.VMEM((2, n_pg, w, ps), F32), pltpu.VMEM((2, n_pg, w, ps), F32),
                        pltpu.SemaphoreType.DMA((2,)), pltpu.SemaphoreType.DMA((2,))],
    )
    return pl.pallas_call(
        functools.partial(_sattn_kernel, n_pg=n_pg, lam_init=lam_init),
        grid_spec=grid_spec,
        out_shape=jax.ShapeDtypeStruct((w, b), F32),
        compiler_params=_params(("arbitrary", "arbitrary", "arbitrary")),
        name="sample_attention",
    )(page_table_t, q_t, kn_t, vn_t, bpage, bnew, da_g.reshape(w, 1), *lams, ck_t, cv_t)


def _outproj_kernel(ap_ref, mop_ref, xp_ref, g1p_ref, sh2p_ref, sc2p_ref, g2p_ref,
                    as_ref, mos_ref, xs_ref, g1s_ref, sh2s_ref, sc2s_ref, g2s_ref,
                    wo_ref, n2_ref, wrt_ref, wsg_ref, wsu_ref, wsd_ref,
                    x1p_ref, x1s_ref, h2t_ref, lt_ref):
    i = pl.program_id(0)
    n_prompt = pl.num_programs(0) - 1

    def block(a, mo, x, g1, sh2, sc2, g2, out_ref):
        half = a.shape[1]
        mix = jnp.dot(a, wo_ref[0:half, :], preferred_element_type=F32)
        mix = mix + jnp.dot(mo, wo_ref[half:2 * half, :], preferred_element_type=F32)
        x1 = x + g1 * mix
        h2 = _modulated_norm(x1, n2_ref[...], sh2, sc2)
        lt_ref[...] = lax.dot_general(wrt_ref[...], h2, NT, precision=HI, preferred_element_type=F32)
        tm = h2.shape[0]
        for s in range(h2.shape[1] // LANES):
            h2t_ref[pl.ds(s, tm, stride=SUBLANES), :] = h2[:, s * LANES:(s + 1) * LANES]
        hb = h2.astype(BF16)
        gate = jnp.dot(hb, wsg_ref[...], preferred_element_type=F32)
        up = jnp.dot(hb, wsu_ref[...], preferred_element_type=F32)
        act = (gate * _sigmoid(gate) * up).astype(BF16)
        shared = jnp.dot(act, wsd_ref[...], preferred_element_type=F32)
        out_ref[...] = x1 + g2 * shared

    @pl.when(i < n_prompt)
    def _():
        block(ap_ref[...], mop_ref[...], xp_ref[...], g1p_ref[...], sh2p_ref[...], sc2p_ref[...], g2p_ref[...],
              x1p_ref)

    @pl.when(i == n_prompt)
    def _():
        block(as_ref[...].T.astype(BF16), mos_ref[...].T.astype(BF16), xs_ref[...], g1s_ref[...], sh2s_ref[...],
              sc2s_ref[...], g2s_ref[...], x1s_ref)


def _outproj(a_p, mo_p, x_p, mod_p, a_ts, mo_ts, x_s, mod_s, w_out, n2, w_router_t, wsg, wsu, wsd):
    t, d = x_p.shape
    bs = x_s.shape[0]
    tm = TOK_BLK
    assert bs == tm and t % tm == 0
    nbp = t // tm
    e = w_router_t.shape[0]
    nsl = d // LANES
    prow = lambda n: pl.BlockSpec((tm, n), lambda i: (jnp.minimum(i, nbp - 1), 0))
    in_specs = ([prow(DA_COLS), prow(ML_COLS), prow(d)] + [_const_spec((1, d))] * 4
                + [_const_spec((DA_COLS, bs)), _const_spec((ML_COLS, bs)), _const_spec((bs, d))]
                + [_const_spec((bs, d))] * 4
                + [_const_spec(w_out.shape), _const_spec((1, d)), _const_spec(w_router_t.shape),
                   _const_spec(wsg.shape), _const_spec(wsu.shape), _const_spec(wsd.shape)])
    return pl.pallas_call(
        _outproj_kernel,
        grid=(nbp + 1,),
        in_specs=in_specs,
        out_specs=[prow(d), _const_spec((bs, d)), pl.BlockSpec((tm * nsl, LANES), lambda i: (i, 0)),
                   pl.BlockSpec((e, tm), lambda i: (0, i))],
        out_shape=[jax.ShapeDtypeStruct((t, d), F32), jax.ShapeDtypeStruct((bs, d), F32),
                   jax.ShapeDtypeStruct(((t + bs) * nsl, LANES), F32), jax.ShapeDtypeStruct((e, t + bs), F32)],
        compiler_params=_params(("arbitrary",)),
        name="outproj",
    )(a_p, mo_p, x_p, *mod_p, a_ts, mo_ts, x_s, *mod_s, w_out, n2, w_router_t, wsg, wsu, wsd)


def _first_max(cur, iota, n, axis=0):
    t = jnp.max(cur, axis=axis, keepdims=True)
    idx = jnp.min(jnp.where(cur == t, iota, float(n)), axis=axis, keepdims=True)
    return t, idx


def _route_kernel(lt_ref, bias_ref, ut_ref, ones_ref, eidx_ref, w_ref, rank_ref, cnt_ref, carry):
    i = pl.program_id(0)

    @pl.when(i == 0)
    def _():
        carry[...] = jnp.zeros_like(carry)

    aff = _sigmoid(lt_ref[...])
    sel = aff + bias_ref[...]
    n_e, n_t = aff.shape
    gsz = n_e // N_GROUPS
    e_iota = lax.broadcasted_iota(I32, (n_e, n_t), 0).astype(F32)
    g_iota = lax.broadcasted_iota(I32, (N_GROUPS, n_t), 0).astype(F32)
    in_iota = lax.broadcasted_iota(I32, (gsz, n_t), 0).astype(F32)
    ninf = -jnp.inf
    gscore = jnp.zeros((N_GROUPS, n_t), F32)
    for g in range(N_GROUPS):
        blk = sel[g * gsz:(g + 1) * gsz, :]
        t1, i1 = _first_max(blk, in_iota, gsz)
        t2 = jnp.max(jnp.where(in_iota == i1, ninf, blk), axis=0, keepdims=True)
        gscore = jnp.where(g_iota == float(g), t1 + t2, gscore)
    gmask = jnp.zeros((N_GROUPS, n_t), F32)
    cur = gscore
    for _ in range(TOPK_GROUPS):
        _, gi = _first_max(cur, g_iota, N_GROUPS)
        hit = g_iota == gi
        gmask = jnp.where(hit, 1.0, gmask)
        cur = jnp.where(hit, ninf, cur)
    cur = jnp.concatenate(
        [jnp.where(gmask[g:g + 1, :] > 0.0, sel[g * gsz:(g + 1) * gsz, :], ninf) for g in range(N_GROUPS)], axis=0)
    hits, ws = [], []
    msum = jnp.zeros((n_e, n_t), F32)
    for k in range(TOP_K):
        _, ei = _first_max(cur, e_iota, n_e)
        hit = e_iota == ei
        ws.append(jnp.sum(jnp.where(hit, aff, 0.0), axis=0, keepdims=True))
        cur = jnp.where(hit, ninf, cur)
        msum = msum + hit.astype(F32)
        hits.append(hit)
        eidx_ref[k:k + 1, :] = ei.astype(I32)
    wsum = ws[0]
    for k in range(1, TOP_K):
        wsum = wsum + ws[k]
    mb = msum.astype(BF16)
    base = carry[...] + jnp.dot(mb, ut_ref[...], preferred_element_type=F32)
    for k in range(TOP_K):
        w_ref[k:k + 1, :] = ws[k] / wsum * ROUTED_SCALE
        rank_ref[k:k + 1, :] = jnp.sum(jnp.where(hits[k], base, 0.0), axis=0, keepdims=True).astype(I32)
    carry[...] = carry[...] + jnp.dot(mb, ones_ref[...], preferred_element_type=F32)
    cnt_ref[...] = carry[...]


def _route(logits_t, router_bias):
    n_e, t_all = logits_t.shape
    tb = TOK_BLK
    ut = jnp.asarray(np.triu(np.ones((tb, tb), np.float32), 1), dtype=BF16)
    ones = jnp.ones((tb, tb), BF16)
    bias_b = jnp.broadcast_to(router_bias.astype(F32)[:, None], (n_e, tb))
    tok = pl.BlockSpec((TOP_K, tb), lambda i: (0, i))
    return pl.pallas_call(
        _route_kernel,
        grid=(t_all // tb,),
        in_specs=[pl.BlockSpec((n_e, tb), lambda i: (0, i)), _const_spec((n_e, tb)), _const_spec((tb, tb)),
                  _const_spec((tb, tb))],
        out_specs=[tok, tok, tok, _const_spec((n_e, tb))],
        out_shape=[jax.ShapeDtypeStruct((TOP_K, t_all), I32), jax.ShapeDtypeStruct((TOP_K, t_all), F32),
                   jax.ShapeDtypeStruct((TOP_K, t_all), I32), jax.ShapeDtypeStruct((n_e, tb), F32)],
        scratch_shapes=[pltpu.VMEM((n_e, tb), F32)],
        compiler_params=_params(("arbitrary",)),
        name="moe_route",
    )(logits_t, bias_b, ut, ones)


def _pos_kernel(eidx_ref, rank_ref, offs_ref, pos_ref):
    offs = offs_ref[...]
    e_iota = lax.broadcasted_iota(I32, offs.shape, 0)
    for k in range(TOP_K):
        hit = e_iota == eidx_ref[k:k + 1, :]
        base = jnp.sum(jnp.where(hit, offs, 0.0), axis=0, keepdims=True).astype(I32)
        pos_ref[k:k + 1, :] = base + rank_ref[k:k + 1, :]


def _positions(eidx, rank, offs):
    t_all = eidx.shape[1]
    tb = TOK_BLK
    n_e = offs.shape[0]
    tok = pl.BlockSpec((TOP_K, tb), lambda i: (0, i))
    return pl.pallas_call(
        _pos_kernel,
        grid=(t_all // tb,),
        in_specs=[tok, tok, _const_spec((n_e, tb))],
        out_specs=tok,
        out_shape=jax.ShapeDtypeStruct((TOP_K, t_all), I32),
        compiler_params=_params(("parallel",)),
        name="moe_positions",
    )(eidx, rank, jnp.broadcast_to(offs.astype(F32)[:, None], (n_e, tb)))


def _tile_start(row):
    start = row * SUBLANES
    return start if isinstance(start, int) else pl.multiple_of(start, SUBLANES)


def _row_copy(src, src_row, dst, dst_row, sem):
    return pltpu.make_async_copy(src.at[pl.ds(_tile_start(src_row), SUBLANES)],
                                 dst.at[pl.ds(_tile_start(dst_row), SUBLANES)], sem)


def _dispatch_kernel(pos_ref, h2t_ref, xs_in_ref, xs_ref, sem):
    del xs_in_ref
    i = pl.program_id(0)
    tb = pos_ref.shape[1]

    def issue(t, carry):
        for k in range(TOP_K):
            _row_copy(h2t_ref, i * tb + t, xs_ref, pos_ref[k, t], sem).start()
        return carry

    lax.fori_loop(0, tb, issue, 0)

    def drain(t, carry):
        for k in range(TOP_K):
            _row_copy(h2t_ref, 0, xs_ref, 0, sem).wait()
        return carry

    lax.fori_loop(0, tb, drain, 0)


def _dispatch(pos, h2t, n_rows):
    t_all = pos.shape[1]
    tb = TOK_BLK
    xs0 = jnp.zeros((n_rows * SUBLANES, LANES), F32)
    return pl.pallas_call(
        _dispatch_kernel,
        grid=(t_all // tb,),
        in_specs=[pl.BlockSpec((TOP_K, tb), lambda i: (0, i), memory_space=pltpu.SMEM),
                  pl.BlockSpec(memory_space=pl.ANY), pl.BlockSpec(memory_space=pl.ANY)],
        out_specs=pl.BlockSpec(memory_space=pl.ANY),
        out_shape=jax.ShapeDtypeStruct(xs0.shape, F32),
        input_output_aliases={2: 0},
        scratch_shapes=[pltpu.SemaphoreType.DMA(())],
        compiler_params=_params(("arbitrary",)),
        name="moe_dispatch",
    )(pos, h2t, xs0)


def _experts_kernel(be_ref, nv_ref, xs_ref, wg_ref, wu_ref, wd_ref, ys_ref):
    i = pl.program_id(0)

    @pl.when(i < nv_ref[0])
    def _():
        bm = xs_ref.shape[0] // SUBLANES
        nsl = wg_ref.shape[0] // LANES
        x = jnp.concatenate([xs_ref[pl.ds(s, bm, stride=SUBLANES), :] for s in range(nsl)], axis=1).astype(BF16)
        gate = jnp.dot(x, wg_ref[...].astype(BF16), preferred_element_type=F32)
        up = jnp.dot(x, wu_ref[...].astype(BF16), preferred_element_type=F32)
        act = (gate * _sigmoid(gate) * up).astype(BF16)
        y = jnp.dot(act, wd_ref[...].astype(BF16), preferred_element_type=F32)
        for s in range(nsl):
            ys_ref[pl.ds(s, bm, stride=SUBLANES), :] = y[:, s * LANES:(s + 1) * LANES]


def _experts(block_e, n_valid, xs, w_g, w_u, w_d, n_blocks):
    bm = MOE_BM
    d, de = w_g.shape[1], w_g.shape[2]
    rows = pl.BlockSpec((bm * SUBLANES, LANES), lambda i, be, nv: (jnp.minimum(i, nv[0] - 1), 0))
    grid_spec = pltpu.PrefetchScalarGridSpec(
        num_scalar_prefetch=2,
        grid=(n_blocks,),
        in_specs=[rows,
                  pl.BlockSpec((None, d, de), lambda i, be, nv: (be[i], 0, 0)),
                  pl.BlockSpec((None, d, de), lambda i, be, nv: (be[i], 0, 0)),
                  pl.BlockSpec((None, de, d), lambda i, be, nv: (be[i], 0, 0))],
        out_specs=rows,
    )
    return pl.pallas_call(
        _experts_kernel,
        grid_spec=grid_spec,
        out_shape=jax.ShapeDtypeStruct(xs.shape, F32),
        compiler_params=_params(("arbitrary",)),
        name="moe_experts",
    )(block_e, n_valid, xs, w_g, w_u, w_d)


def _combine_kernel(pos_ref, w_ref, x1p_ref, g2p_ref, x1s_ref, g2s_ref, fg_ref, ys_ref, yp_ref, ysm_ref, buf, sem):
    i = pl.program_id(0)
    n_prompt = pl.num_programs(0) - 1
    tb = pos_ref.shape[1]
    nsl = x1p_ref.shape[1] // LANES

    def issue(t, carry):
        for k in range(TOP_K):
            pltpu.make_async_copy(ys_ref.at[pl.ds(_tile_start(pos_ref[k, t]), SUBLANES)],
                                  buf.at[k, pl.ds(_tile_start(t), SUBLANES)], sem).start()
        return carry

    lax.fori_loop(0, tb, issue, 0)

    def drain(t, carry):
        for k in range(TOP_K):
            pltpu.make_async_copy(ys_ref.at[pl.ds(0, SUBLANES)], buf.at[k, pl.ds(0, SUBLANES)], sem).wait()
        return carry

    lax.fori_loop(0, tb, drain, 0)

    w = w_ref[...]
    routed = None
    for k in range(TOP_K):
        yk = jnp.concatenate([buf[k, pl.ds(s, tb, stride=SUBLANES), :] for s in range(nsl)], axis=1)
        term = yk * w[:, k:k + 1]
        routed = term if routed is None else routed + term

    def finish(x1, g2, out_ref):
        x2 = x1 + g2 * routed
        out_ref[...] = x2 * lax.rsqrt(jnp.mean(x2 * x2, axis=-1, keepdims=True) + EPS) * fg_ref[...]

    @pl.when(i < n_prompt)
    def _():
        finish(x1p_ref[...], g2p_ref[...], yp_ref)

    @pl.when(i == n_prompt)
    def _():
        finish(x1s_ref[...], g2s_ref[...], ysm_ref)


def _combine(pos, w_tm, x1_p, g2_p, x1_s, g2_s, final_g, ys):
    t, d = x1_p.shape
    bs = x1_s.shape[0]
    tb = TOK_BLK
    assert bs == tb and t % tb == 0
    nbp = t // tb
    prow = pl.BlockSpec((tb, d), lambda i: (jnp.minimum(i, nbp - 1), 0))
    return pl.pallas_call(
        _combine_kernel,
        grid=(nbp + 1,),
        in_specs=[pl.BlockSpec((TOP_K, tb), lambda i: (0, i), memory_space=pltpu.SMEM),
                  pl.BlockSpec((tb, TOP_K), lambda i: (i, 0)),
                  prow, _const_spec((1, d)), _const_spec((bs, d)), _const_spec((bs, d)), _const_spec((1, d)),
                  pl.BlockSpec(memory_space=pl.ANY)],
        out_specs=[prow, _const_spec((bs, d))],
        out_shape=[jax.ShapeDtypeStruct((t, d), F32), jax.ShapeDtypeStruct((bs, d), F32)],
        scratch_shapes=[pltpu.VMEM((TOP_K, tb * SUBLANES, LANES), F32), pltpu.SemaphoreType.DMA(())],
        compiler_params=_params(("arbitrary",)),
        name="moe_combine",
    )(pos, w_tm, x1_p, g2_p, x1_s, g2_s, final_g.reshape(1, d), ys)


def kernel(x_prompt, x_sample, cache_k, cache_v, state_C, state_n, state_m, state_conv, page_table, c_prompt, c_sample, w_ada, b_ada, norm1_g, norm2_g, w_in, conv_w, conv_b, lambda_q1, lambda_k1, lambda_q2, lambda_k2, da_norm_g, rel_bias, ml_b_i, ml_b_f, ml_norm_g, w_out, w_router, router_bias, w_exp_gate, w_exp_up, w_exp_down, w_sh_gate, w_sh_up, w_sh_down, final_norm_g):
    bp, t, d = x_prompt.shape
    bs, ts, _ = x_sample.shape
    depth = w_ada.shape[0]
    assert bp == 1 and ts == 1 and depth == 1
    assert d % LANES == 0 and (t + bs) % TOK_BLK == 0 and t % TOK_BLK == 0
    l = 0
    lam_init = 0.8 - 0.6 * math.exp(-0.3 * l)
    lams = [a[l].reshape(1, -1).astype(F32) for a in (lambda_q1, lambda_k1, lambda_q2, lambda_k2)]
    t_all = t + bs
    hd = ML_HEAD_DIM

    pad = (-(bp + bs)) % SUBLANES
    c_all = jnp.concatenate([c_prompt, c_sample, jnp.zeros((pad, d), F32)], axis=0)
    mod = _adaln(c_all, w_ada[l], b_ada[l])
    modp = [mod[0:1, i * d:(i + 1) * d] for i in range(6)]
    mods = [mod[1:1 + bs, i * d:(i + 1) * d] for i in range(6)]

    w_in_t = w_in[l].T
    c = DA_COLS
    wqkv_t = w_in_t[0:3 * c].astype(BF16)
    wml_t = w_in_t[3 * c:3 * c + 4 * ML_COLS].astype(BF16)
    wg_t = w_in_t[3 * c + 4 * ML_COLS:]
    b_gate = jnp.concatenate([ml_b_i[l], ml_b_f[l]]).astype(F32)
    w_out_b = w_out[l].astype(BF16)
    w_router_t = w_router[l].T
    wsg, wsu, wsd = w_sh_gate[l].astype(BF16), w_sh_up[l].astype(BF16), w_sh_down[l].astype(BF16)
    n1 = norm1_g[l].reshape(1, d)
    n2 = norm2_g[l].reshape(1, d)

    xp = x_prompt.reshape(t, d)
    (qbf, ktf, ktb, vtf, vtb, mqk, mv, mo_raw, gates, gates_t) = _inproj(
        xp, modp[0], modp[1], n1, wqkv_t[0:c], wqkv_t[c:3 * c], wml_t, wg_t, b_gate)
    attn_p = _prompt_attention(qbf, ktb, vtb, rel_bias, da_norm_g[l], lams, lam_init)
    mo_p, s_p, m_p = _mlstm_prompt(mqk, mv, mo_raw, gates, gates_t, conv_w[l], conv_b[l], ml_norm_g[l])

    xs = x_sample.reshape(bs, d)
    qkvt_s, mqk_s, mvot_s, gates_ts = _sample_inproj(xs, mods[0], mods[1], n1, wqkv_t, wml_t, wg_t, b_gate)
    conv_buf = jnp.transpose(state_conv[l], (1, 0, 2))
    qkt_s, conv_new_s = _sample_conv(mqk_s, conv_buf, conv_w[l], conv_b[l])
    c_t = jnp.transpose(state_C[l], (1, 2, 3, 0))
    n_t = jnp.transpose(state_n[l], (1, 2, 0))
    m_t = state_m[l].T
    c_s, n_s, m_s, mo_ts = _sample_mlstm(qkt_s, mvot_s[0:ML_COLS], mvot_s[ML_COLS:2 * ML_COLS], gates_ts, m_t, c_t,
                                         n_t, ml_norm_g[l])
    n_pool = cache_k.shape[1]
    ck_t = jnp.transpose(cache_k[l], (0, 2, 3, 1)).reshape(n_pool, c, PAGE_SIZE)
    cv_t = jnp.transpose(cache_v[l], (0, 2, 3, 1)).reshape(n_pool, c, PAGE_SIZE)
    attn_ts = _sample_attention(qkvt_s[0:c], qkvt_s[c:2 * c], qkvt_s[2 * c:3 * c], ck_t, cv_t,
                                page_table.T, rel_bias, da_norm_g[l], lams, lam_init)

    xs1_p, xs1_s, h2t, logits_t = _outproj(attn_p, mo_p, xp, modp[2:6], attn_ts, mo_ts, xs, mods[2:6], w_out_b, n2,
                                           w_router_t, wsg, wsu, wsd)

    n_e = w_router.shape[2]
    eidx, wts, rank, cnt = _route(logits_t, router_bias[l])
    counts = cnt[:, 0].astype(I32)
    padded = (counts + MOE_BM - 1) // MOE_BM * MOE_BM
    pad_end = jnp.cumsum(padded)
    offs = pad_end - padded
    n_blocks = -(-(t_all * TOP_K) // MOE_BM) + n_e
    block_e = jnp.minimum(jnp.searchsorted(pad_end, jnp.arange(n_blocks) * MOE_BM, side='right'),
                          n_e - 1).astype(I32)
    n_valid = (pad_end[-1:] // MOE_BM).astype(I32)
    pos = _positions(eidx, rank, offs)
    xs_sorted = _dispatch(pos, h2t, n_blocks * MOE_BM)
    ys = _experts(block_e, n_valid, xs_sorted, w_exp_gate[l], w_exp_up[l], w_exp_down[l], n_blocks)
    w_tm = wts.T
    y_p, y_s = _combine(pos, w_tm, xs1_p, modp[5], xs1_s, mods[5], final_norm_g, ys)

    def per_head(a_t, n_feat):
        return jnp.transpose(a_t.reshape(-1, n_feat, a_t.shape[1]), (2, 0, 1))

    return (
        y_p.reshape(bp, t, d),
        y_s.reshape(bs, ts, d),
        per_head(ktf, 2 * DA_QK_DIM).reshape(1, bp, t, DA_HEADS, 2 * DA_QK_DIM),
        per_head(vtf, DA_V_DIM).reshape(1, bp, t, DA_HEADS, DA_V_DIM),
        s_p[:, :, :hd].reshape(1, bp, ML_HEADS, hd, hd),
        s_p[:, :, hd].reshape(1, bp, ML_HEADS, hd),
        m_p.reshape(1, bp, ML_HEADS),
        mqk[t - (CONV_W - 1):, :].reshape(1, bp, CONV_W - 1, 2 * ML_COLS),
        per_head(qkvt_s[c:2 * c], 2 * DA_QK_DIM).reshape(1, bs, ts, DA_HEADS, 2 * DA_QK_DIM),
        per_head(qkvt_s[2 * c:3 * c], DA_V_DIM).reshape(1, bs, ts, DA_HEADS, DA_V_DIM),
        jnp.transpose(c_s, (3, 0, 1, 2)).reshape(1, bs, ML_HEADS, hd, hd),
        jnp.transpose(n_s, (2, 0, 1)).reshape(1, bs, ML_HEADS, hd),
        m_s.reshape(ML_HEADS, bs).T.reshape(1, bs, ML_HEADS),
        jnp.transpose(conv_new_s, (1, 0, 2)).reshape(1, bs, CONV_W - 1, 2 * ML_COLS),
    )
```

```python
import functools
import math

import jax
import jax.numpy as jnp
import numpy as np
from jax import lax
from jax.experimental import pallas as pl
from jax.experimental.pallas import tpu as pltpu

F32 = jnp.float32
BF16 = jnp.bfloat16
I32 = jnp.int32
HI = lax.Precision.HIGHEST

DA_HEADS = 8
DA_QK_DIM = 32
DA_V_DIM = 64
ML_HEADS = 8
ML_HEAD_DIM = 64
CONV_W = 4
NUM_BUCKETS = 32
MAX_DISTANCE = 128
PAGE_SIZE = 128
N_GROUPS = 8
TOPK_GROUPS = 4
TOP_K = 8
ROUTED_SCALE = 2.5
EPS = 1e-6

DA_COLS = DA_HEADS * 2 * DA_QK_DIM
ML_COLS = ML_HEADS * ML_HEAD_DIM
N_GATES = 2 * ML_HEADS
Q_SCALE = DA_QK_DIM ** -0.5
K_SCALE = ML_HEAD_DIM ** -0.5
LOG2E = 1.0 / math.log(2.0)
NEG = -1e30

LANES = 128
SUBLANES = 8
VMEM_LIMIT = 56 * 1024 * 1024

ML_CHUNK = 128
ATT_TILE = 512
MOE_BM = 128
TOK_BLK = 128
PAGES_PER_STEP = 16
PROJ_TM = 256

NT = (((1,), (1,)), ((), ()))


def _const_spec(shape):
    nd = len(shape)
    return pl.BlockSpec(shape, lambda *_: (0,) * nd)


def _params(sem, vmem=VMEM_LIMIT):
    return pltpu.CompilerParams(dimension_semantics=sem, vmem_limit_bytes=vmem)


def _sigmoid(x):
    return 1.0 / (1.0 + jnp.exp(-x))


def _log_sigmoid(x):
    return jnp.minimum(x, 0.0) - jnp.log1p(jnp.exp(-jnp.abs(x)))


def _lam(lq1, lk1, lq2, lk2, lam_init):
    a = jnp.sum(lq1[...] * lk1[...], axis=-1, keepdims=True)
    b = jnp.sum(lq2[...] * lk2[...], axis=-1, keepdims=True)
    return jnp.exp(a) - jnp.exp(b) + lam_init


def _modulated_norm(x, g, shift, scale):
    y = x * lax.rsqrt(jnp.mean(x * x, axis=-1, keepdims=True) + EPS) * g
    return y * (1.0 + scale) + shift


def _t5_bucket(dist):
    n = jnp.maximum(dist, 0)
    max_exact = NUM_BUCKETS // 2
    nf = jnp.maximum(n, 1).astype(F32)
    large = max_exact + (jnp.log(nf / max_exact) / math.log(MAX_DISTANCE / max_exact)
                         * (NUM_BUCKETS - max_exact)).astype(I32)
    return jnp.where(n < max_exact, n, jnp.minimum(large, NUM_BUCKETS - 1))


def _bias_by_distance(rel_bias, n):
    rb = rel_bias.astype(F32)
    return rb[_t5_bucket(jnp.arange(n))] - rb[NUM_BUCKETS - 1][None, :]


def _adaln_kernel(c_ref, w_ref, b_ref, o_ref):
    c = c_ref[...]
    a = c * _sigmoid(c)
    o_ref[...] = jnp.dot(a, w_ref[...], precision=HI, preferred_element_type=F32) + b_ref[...]


def _adaln(c_all, w, b):
    r, d = c_all.shape
    n = w.shape[1]
    tn = n // 4
    return pl.pallas_call(
        _adaln_kernel,
        grid=(n // tn,),
        in_specs=[_const_spec((r, d)), pl.BlockSpec((d, tn), lambda j: (0, j)),
                  pl.BlockSpec((1, tn), lambda j: (0, j))],
        out_specs=pl.BlockSpec((r, tn), lambda j: (0, j)),
        out_shape=jax.ShapeDtypeStruct((r, n), F32),
        compiler_params=_params(("parallel",)),
        name="adaln",
    )(c_all, w, b.reshape(1, n))


def _inproj_kernel(x_ref, sh_ref, sc_ref, g_ref, wq_ref, wkv_ref, wml_ref, wg_ref, bg_ref, bgt_ref,
                   qbf_ref, ktf_ref, ktb_ref, vtf_ref, vtb_ref, mqk_ref, mv_ref, mo_ref, gates_ref, gatest_ref):
    h = _modulated_norm(x_ref[...], g_ref[...], sh_ref[...], sc_ref[...])
    hb = h.astype(BF16)
    q = lax.dot_general(hb, wq_ref[...], NT, preferred_element_type=F32)
    qbf_ref[...] = (q * (Q_SCALE * LOG2E)).astype(BF16)
    kvt = lax.dot_general(wkv_ref[...], hb, NT, preferred_element_type=F32)
    ktf_ref[...] = kvt[0:DA_COLS, :]
    ktb_ref[...] = kvt[0:DA_COLS, :].astype(BF16)
    vtf_ref[...] = kvt[DA_COLS:2 * DA_COLS, :]
    vtb_ref[...] = kvt[DA_COLS:2 * DA_COLS, :].astype(BF16)
    ml = lax.dot_general(hb, wml_ref[...], NT, preferred_element_type=F32)
    mqk_ref[...] = ml[:, 0:2 * ML_COLS]
    mv_ref[...] = ml[:, 2 * ML_COLS:3 * ML_COLS].astype(BF16)
    mo_ref[...] = ml[:, 3 * ML_COLS:4 * ML_COLS]
    graw = lax.dot_general(h, wg_ref[...], NT, precision=HI, preferred_element_type=F32) + bg_ref[...]
    lane = lax.broadcasted_iota(I32, graw.shape, 1)
    gates_ref[...] = jnp.where(lane < ML_HEADS, graw, _log_sigmoid(graw))
    grawt = lax.dot_general(wg_ref[...], h, NT, precision=HI, preferred_element_type=F32) + bgt_ref[...]
    sub = lax.broadcasted_iota(I32, grawt.shape, 0)
    gatest_ref[...] = jnp.where(sub < ML_HEADS, grawt, _log_sigmoid(grawt))


def _inproj(x, shift, scale, g, wq_t, wkv_t, wml_t, wg_t, b_gate):
    r, d = x.shape
    tm = min(PROJ_TM, r)
    row = lambda n: pl.BlockSpec((tm, n), lambda i: (i, 0))
    col = lambda n: pl.BlockSpec((n, tm), lambda i: (0, i))
    ng = N_GATES
    out_shape = [
        jax.ShapeDtypeStruct((r, DA_COLS), BF16),
        jax.ShapeDtypeStruct((DA_COLS, r), F32), jax.ShapeDtypeStruct((DA_COLS, r), BF16),
        jax.ShapeDtypeStruct((DA_COLS, r), F32), jax.ShapeDtypeStruct((DA_COLS, r), BF16),
        jax.ShapeDtypeStruct((r, 2 * ML_COLS), F32), jax.ShapeDtypeStruct((r, ML_COLS), BF16),
        jax.ShapeDtypeStruct((r, ML_COLS), F32),
        jax.ShapeDtypeStruct((r, ng), F32), jax.ShapeDtypeStruct((ng, r), F32),
    ]
    out_specs = [row(DA_COLS), col(DA_COLS), col(DA_COLS), col(DA_COLS), col(DA_COLS), row(2 * ML_COLS),
                 row(ML_COLS), row(ML_COLS), row(ng), col(ng)]
    return pl.pallas_call(
        _inproj_kernel,
        grid=(r // tm,),
        in_specs=[row(d), _const_spec((1, d)), _const_spec((1, d)), _const_spec((1, d)),
                  _const_spec(wq_t.shape), _const_spec(wkv_t.shape), _const_spec(wml_t.shape),
                  _const_spec(wg_t.shape), _const_spec((1, ng)), _const_spec((ng, 1))],
        out_specs=out_specs,
        out_shape=out_shape,
        compiler_params=_params(("parallel",)),
        name="inproj",
    )(x, shift, scale, g, wq_t, wkv_t, wml_t, wg_t, b_gate.reshape(1, ng), b_gate.reshape(ng, 1))


def _pair_rmsnorm(x, lane, nd):
    sq = x * x
    lo = jnp.sum(jnp.where(lane < nd, sq, 0.0), axis=-1, keepdims=True) * (1.0 / nd)
    hi = jnp.sum(jnp.where(lane >= nd, sq, 0.0), axis=-1, keepdims=True) * (1.0 / nd)
    return x * jnp.where(lane < nd, lax.rsqrt(lo + EPS), lax.rsqrt(hi + EPS))


def _mlstm_kernel(mqk_ref, mv_ref, og_ref, gates_ref, gatest_ref, cw_ref, cb_ref, mlg_ref,
                  tri_ref, trit_ref, mo_ref, sout_ref, mout_ref, xprev, st, msc):
    c = pl.program_id(0)
    L = mqk_ref.shape[0]
    hd = ML_HEAD_DIM

    @pl.when(c == 0)
    def _():
        xprev[...] = jnp.zeros_like(xprev)
        st[...] = jnp.zeros_like(st)
        msc[...] = jnp.zeros_like(msc)

    x = mqk_ref[...]
    xp = xprev[...]
    row = lax.broadcasted_iota(I32, x.shape, 0)

    def shifted(j):
        return jnp.where(row < j, pltpu.roll(xp, j, 0), pltpu.roll(x, j, 0))

    y = cb_ref[...] + shifted(3) * cw_ref[0:1, :]
    y = y + shifted(2) * cw_ref[1:2, :]
    y = y + shifted(1) * cw_ref[2:3, :]
    y = y + x * cw_ref[3:4, :]
    qk = y * _sigmoid(y)
    xprev[...] = x

    g = gates_ref[...]
    gt = gatest_ref[...]
    fcol_all = jnp.dot(tri_ref[...], g, precision=HI, preferred_element_type=F32)
    frow_all = jnp.dot(gt, trit_ref[...], precision=HI, preferred_element_type=F32)
    r_i = lax.broadcasted_iota(I32, (L, L), 0)
    c_i = lax.broadcasted_iota(I32, (L, L), 1)
    causal = c_i <= r_i
    lane = lax.broadcasted_iota(I32, (L, 2 * hd), 1)

    for p in range(ML_HEADS // 2):
        qpair = qk[:, 2 * hd * p:2 * hd * (p + 1)]
        kpair = qk[:, ML_COLS + 2 * hd * p:ML_COLS + 2 * hd * (p + 1)] * K_SCALE
        vpair = mv_ref[:, 2 * hd * p:2 * hd * (p + 1)].astype(F32)
        hh, decays, vaugs, carries = [], [], [], []
        for j in range(2):
            h = 2 * p + j
            q_h = qpair[:, hd * j:hd * (j + 1)].astype(BF16)
            k_h = kpair[:, hd * j:hd * (j + 1)].astype(BF16)
            s_raw = lax.dot_general(q_h, k_h, NT, preferred_element_type=F32)
            fc = fcol_all[:, ML_HEADS + h:ML_HEADS + h + 1]
            fr = frow_all[ML_HEADS + h:ML_HEADS + h + 1, :]
            igr = gt[h:h + 1, :]
            igc = g[:, h:h + 1]
            m_old = msc[h]
            dlog = jnp.where(causal, fc - fr + igr, NEG)
            inter = m_old + fc
            m_t = jnp.maximum(inter, jnp.max(dlog, axis=-1, keepdims=True))
            dmat = jnp.where(causal, jnp.exp(dlog - m_t), 0.0)
            smat = (s_raw * dmat).astype(BF16)
            vsrc = vpair if j == 0 else pltpu.roll(vpair, hd, 1)
            vaug = jnp.where(lane < hd, vsrc, jnp.where(lane == hd, 1.0, 0.0)).astype(BF16)
            s_h = st[h]
            q_s = jnp.dot(q_h, s_h.astype(BF16), preferred_element_type=F32)
            num = jnp.dot(smat, vaug, preferred_element_type=F32) + jnp.exp(inter - m_t) * q_s
            den = num[:, hd:hd + 1]
            hh.append(num / jnp.maximum(jnp.abs(den), jnp.exp(-m_t)))
            m_new = m_t[L - 1:L, :]
            f_last = fc[L - 1:L, :]
            decays.append(jnp.exp(f_last - fc + igc - m_new))
            carries.append(jnp.exp(m_old + f_last - m_new))
            vaugs.append(vaug)
            msc[h] = m_new
        kd = kpair * jnp.where(lane < hd, decays[0], decays[1])
        kdt = kd.T
        for j in range(2):
            h = 2 * p + j
            upd = jnp.dot(kdt[hd * j:hd * (j + 1), :].astype(BF16), vaugs[j], preferred_element_type=F32)
            st[h] = carries[j] * st[h] + upd
        hpair = jnp.where(lane < hd, hh[0], pltpu.roll(hh[1], hd, 1))
        hn = _pair_rmsnorm(hpair, lane, hd) * mlg_ref[:, 2 * hd * p:2 * hd * (p + 1)]
        og = _sigmoid(og_ref[:, 2 * hd * p:2 * hd * (p + 1)])
        mo_ref[:, 2 * hd * p:2 * hd * (p + 1)] = (hn * og).astype(BF16)

    sout_ref[...] = st[...]
    mout_ref[...] = msc[...]


def _mlstm_prompt(mqk, mv, og, gates, gates_t, conv_w, conv_b, ml_g):
    t = mqk.shape[0]
    L = min(ML_CHUNK, t)
    hd = ML_HEAD_DIM
    tri = jnp.asarray(np.tril(np.ones((L, L), np.float32)))
    row = lambda n: pl.BlockSpec((L, n), lambda i: (i, 0))
    ng = N_GATES
    return pl.pallas_call(
        _mlstm_kernel,
        grid=(t // L,),
        in_specs=[row(2 * ML_COLS), row(ML_COLS), row(ML_COLS), row(ng),
                  pl.BlockSpec((ng, L), lambda i: (0, i)),
                  _const_spec((CONV_W, 2 * ML_COLS)), _const_spec((1, 2 * ML_COLS)),
                  _const_spec((1, ML_COLS)), _const_spec((L, L)), _const_spec((L, L))],
        out_specs=[row(ML_COLS), _const_spec((ML_HEADS, hd, 2 * hd)), _const_spec((ML_HEADS, 1, 1))],
        out_shape=[jax.ShapeDtypeStruct((t, ML_COLS), BF16),
                   jax.ShapeDtypeStruct((ML_HEADS, hd, 2 * hd), F32),
                   jax.ShapeDtypeStruct((ML_HEADS, 1, 1), F32)],
        scratch_shapes=[pltpu.VMEM((L, 2 * ML_COLS), F32), pltpu.VMEM((ML_HEADS, hd, 2 * hd), F32),
                        pltpu.VMEM((ML_HEADS, 1, 1), F32)],
        compiler_params=_params(("arbitrary",)),
        name="mlstm_prompt",
    )(mqk, mv, og, gates, gates_t, conv_w, conv_b.reshape(1, -1), ml_g.reshape(1, -1), tri, tri.T)


def _pattn_kernel(ii_ref, jj_ref, q_ref, kt_ref, vt_ref, bias_ref, dag_ref, lq1, lk1, lq2, lk2,
                  o_ref, m_sc, acc_sc, *, lam_init):
    s_id = pl.program_id(1)
    i = ii_ref[s_id]
    j = jj_ref[s_id]
    dq = DA_QK_DIM
    dv = DA_V_DIM
    den_lane = (dv, 0)

    @pl.when(j == 0)
    def _():
        m_sc[...] = jnp.full_like(m_sc, NEG)
        acc_sc[...] = jnp.zeros_like(acc_sc)

    def step(use_bias):
        q = q_ref[...]
        kt = kt_ref[...]
        vt = vt_ref[...].astype(F32)
        sub_v = lax.broadcasted_iota(I32, vt.shape, 0)
        ones_row = ((sub_v == den_lane[0]).astype(F32), (sub_v == den_lane[1]).astype(F32))
        vt_aug = (jnp.where(sub_v < dv, vt, ones_row[0]).astype(BF16),
                  jnp.where(sub_v >= dv, vt, ones_row[1]).astype(BF16))
        reps = kt.shape[1] // LANES
        for hp in range(2):
            for m in range(2):
                c0 = hp * 2 * dq + m * dq
                s = jnp.dot(q[:, c0:c0 + dq], kt[c0:c0 + dq, :], preferred_element_type=F32)
                if use_bias:
                    s = s + bias_ref[hp]
                idx = 2 * hp + m
                m_prev = m_sc[idx]
                m_new = jnp.maximum(m_prev, jnp.max(s, axis=-1, keepdims=True))
                p = jnp.exp2(s - jnp.concatenate([m_new] * reps, axis=1))
                alpha = jnp.exp2(m_prev - m_new)
                m_sc[idx] = m_new
                acc_sc[idx] = acc_sc[idx] * alpha + lax.dot_general(p.astype(BF16), vt_aug[hp], NT,
                                                                    preferred_element_type=F32)

    near = j >= i - 1
    pl.when(near)(lambda: step(True))
    pl.when(jnp.logical_not(near))(lambda: step(False))

    @pl.when(j == i)
    def _():
        lane_a = lax.broadcasted_iota(I32, acc_sc.shape[1:], 1)
        lam = _lam(lq1, lk1, lq2, lk2, lam_init)
        outs = []
        for hp in range(2):
            a1 = acc_sc[2 * hp]
            a2 = acc_sc[2 * hp + 1]
            dl = den_lane[hp]
            outs.append(a1 / a1[:, dl:dl + 1] - lam * (a2 / a2[:, dl:dl + 1]))
        o = jnp.where(lane_a < dv, outs[0], outs[1])
        o = _pair_rmsnorm(o, lane_a, dv) * dag_ref[...]
        o_ref[...] = (o * (1.0 - lam_init)).astype(BF16)


def _prompt_bias_tiles(rel_bias, tile):
    rb = rel_bias.astype(F32)
    rb = (rb - rb[NUM_BUCKETS - 1][None, :]) * LOG2E
    r = jnp.arange(tile)[:, None]
    c = jnp.arange(tile)[None, :]
    dist = jnp.stack([tile + r - c, r - c], axis=0)
    bucket = _t5_bucket(dist)[None]
    tiles = jnp.zeros((DA_HEADS, 2, tile, tile), F32)
    for b in range(NUM_BUCKETS - 1):
        tiles = jnp.where(bucket == b, rb[b][:, None, None, None], tiles)
    tiles = jnp.where((dist >= 0)[None], tiles, NEG)
    return tiles.reshape(DA_HEADS // 2, 2, 2, tile, tile)


def _prompt_attention(q_bf, kt_bf, vt_bf, rel_bias, da_g, lams, lam_init):
    t = q_bf.shape[0]
    tile = min(ATT_TILE, t)
    assert tile > MAX_DISTANCE or tile == t
    nq = t // tile
    ii, jj = [], []
    for i in range(nq):
        for j in range(i + 1):
            ii.append(i)
            jj.append(j)
    ii = jnp.asarray(np.array(ii, np.int32))
    jj = jnp.asarray(np.array(jj, np.int32))
    bias = _prompt_bias_tiles(rel_bias, tile)
    w = 2 * DA_V_DIM
    lam_specs = [_const_spec((1, DA_QK_DIM))] * 4
    grid_spec = pltpu.PrefetchScalarGridSpec(
        num_scalar_prefetch=2,
        grid=(DA_HEADS // 2, ii.shape[0]),
        in_specs=[
            pl.BlockSpec((tile, w), lambda p, s, ii, jj: (ii[s], p)),
            pl.BlockSpec((w, tile), lambda p, s, ii, jj: (p, jj[s])),
            pl.BlockSpec((w, tile), lambda p, s, ii, jj: (p, jj[s])),
            pl.BlockSpec((None, 2, None, tile, tile),
                         lambda p, s, ii, jj: (p, 0, jnp.clip(jj[s] - ii[s] + 1, 0, 1), 0, 0)),
            pl.BlockSpec((1, w), lambda p, s, ii, jj: (0, p)),
        ] + lam_specs,
        out_specs=pl.BlockSpec((tile, w), lambda p, s, ii, jj: (ii[s], p)),
        scratch_shapes=[pltpu.VMEM((4, tile, LANES), F32), pltpu.VMEM((4, tile, w), F32)],
    )
    return pl.pallas_call(
        functools.partial(_pattn_kernel, lam_init=lam_init),
        grid_spec=grid_spec,
        out_shape=jax.ShapeDtypeStruct((t, DA_COLS), BF16),
        compiler_params=_params(("parallel", "arbitrary")),
        name="prompt_attention",
    )(ii, jj, q_bf, kt_bf, vt_bf, bias, da_g.reshape(1, -1), *lams)


def _sinproj_kernel(x_ref, sh_ref, sc_ref, g_ref, wqkv_ref, wml_ref, wg_ref, bgt_ref,
                    qkvt_ref, mqk_ref, mvot_ref, gatest_ref):
    h = _modulated_norm(x_ref[...], g_ref[...], sh_ref[...], sc_ref[...])
    hb = h.astype(BF16)
    qkvt_ref[...] = lax.dot_general(wqkv_ref[...], hb, NT, preferred_element_type=F32)
    mqk_ref[...] = lax.dot_general(hb, wml_ref[0:2 * ML_COLS, :], NT, preferred_element_type=F32)
    mvot_ref[...] = lax.dot_general(wml_ref[2 * ML_COLS:4 * ML_COLS, :], hb, NT, preferred_element_type=F32)
    grawt = lax.dot_general(wg_ref[...], h, NT, precision=HI, preferred_element_type=F32) + bgt_ref[...]
    sub = lax.broadcasted_iota(I32, grawt.shape, 0)
    gatest_ref[...] = jnp.where(sub < ML_HEADS, grawt, _log_sigmoid(grawt))


def _sample_inproj(x, shift, scale, g, wqkv_t, wml_t, wg_t, b_gate):
    b, d = x.shape
    return pl.pallas_call(
        _sinproj_kernel,
        out_shape=[jax.ShapeDtypeStruct((3 * DA_COLS, b), F32), jax.ShapeDtypeStruct((b, 2 * ML_COLS), F32),
                   jax.ShapeDtypeStruct((2 * ML_COLS, b), F32), jax.ShapeDtypeStruct((N_GATES, b), F32)],
        compiler_params=pltpu.CompilerParams(vmem_limit_bytes=VMEM_LIMIT),
        name="sample_inproj",
    )(x, shift, scale, g, wqkv_t, wml_t, wg_t, b_gate.reshape(N_GATES, 1))


def _sconv_kernel(mqk_ref, buf_ref, cw_ref, cb_ref, qkt_ref, new_ref):
    x = mqk_ref[...]
    y = cb_ref[...] + buf_ref[0] * cw_ref[0:1, :]
    y = y + buf_ref[1] * cw_ref[1:2, :]
    y = y + buf_ref[2] * cw_ref[2:3, :]
    y = y + x * cw_ref[3:4, :]
    a = y * _sigmoid(y)
    lane = lax.broadcasted_iota(I32, a.shape, 1)
    qkt_ref[...] = jnp.where(lane < ML_COLS, a, a * K_SCALE).T
    new_ref[0] = buf_ref[1]
    new_ref[1] = buf_ref[2]
    new_ref[2] = x


def _sample_conv(mqk, conv_buf, conv_w, conv_b):
    b, n = mqk.shape
    return pl.pallas_call(
        _sconv_kernel,
        out_shape=[jax.ShapeDtypeStruct((n, b), F32), jax.ShapeDtypeStruct(conv_buf.shape, F32)],
        compiler_params=pltpu.CompilerParams(vmem_limit_bytes=VMEM_LIMIT),
        name="sample_conv",
    )(mqk, conv_buf, conv_w, conv_b.reshape(1, -1))


def _smlstm_kernel(q_ref, k_ref, v_ref, og_ref, ig_ref, lf_ref, m_ref, c_ref, n_ref, mlg_ref,
                   co_ref, no_ref, mo_ref, ho_ref):
    hd = ML_HEAD_DIM
    ig = ig_ref[...]
    lf = lf_ref[...]
    m_old = m_ref[...]
    m_t = jnp.maximum(m_old + lf, ig)
    a = jnp.exp(ig - m_t)
    w = jnp.exp(m_old + lf - m_t)
    v = v_ref[...]

    def body(d, num):
        cn = w * c_ref[d] + (a * k_ref[pl.ds(d, 1), :]) * v
        co_ref[d] = cn
        return num + q_ref[pl.ds(d, 1), :] * cn

    num = lax.fori_loop(0, hd, body, jnp.zeros(v.shape, F32))
    nn = w * n_ref[...] + a * k_ref[...]
    den = jnp.sum(q_ref[...] * nn, axis=0, keepdims=True)
    hv = num / jnp.maximum(jnp.abs(den), jnp.exp(-m_t))
    hn = hv * lax.rsqrt(jnp.mean(hv * hv, axis=0, keepdims=True) + EPS) * mlg_ref[...]
    no_ref[...] = nn
    mo_ref[...] = m_t
    ho_ref[...] = hn * _sigmoid(og_ref[...])


def _sample_mlstm(qk_t, v_t, og_t, gates_t, m_t, c_t, n_t, ml_g):
    b = qk_t.shape[1]
    h, d = ML_HEADS, ML_HEAD_DIM
    head_rows = lambda off: pl.BlockSpec((d, b), lambda i: (i + off, 0))
    vec = pl.BlockSpec((None, 1, b), lambda i: (i, 0, 0))
    return pl.pallas_call(
        _smlstm_kernel,
        grid=(h,),
        in_specs=[head_rows(0), head_rows(h), head_rows(0), head_rows(0),
                  vec, pl.BlockSpec((None, 1, b), lambda i: (i + h, 0, 0)), vec,
                  pl.BlockSpec((None, d, d, b), lambda i: (i, 0, 0, 0)),
                  pl.BlockSpec((None, d, b), lambda i: (i, 0, 0)),
                  pl.BlockSpec((None, d, 1), lambda i: (i, 0, 0))],
        out_specs=[pl.BlockSpec((None, d, d, b), lambda i: (i, 0, 0, 0)),
                   pl.BlockSpec((None, d, b), lambda i: (i, 0, 0)), vec, head_rows(0)],
        out_shape=[jax.ShapeDtypeStruct((h, d, d, b), F32), jax.ShapeDtypeStruct((h, d, b), F32),
                   jax.ShapeDtypeStruct((h, 1, b), F32), jax.ShapeDtypeStruct((h * d, b), F32)],
        compiler_params=_params(("parallel",)),
        name="sample_mlstm",
    )(qk_t, qk_t, v_t, og_t, gates_t.reshape(2 * h, 1, b), gates_t.reshape(2 * h, 1, b),
      m_t.reshape(h, 1, b), c_t, n_t, ml_g.reshape(h, d, 1))


def _sattn_kernel(pt_ref, qt_ref, knt_ref, vnt_ref, bpage_ref, bnew_ref, dag_ref, lq1, lk1, lq2, lk2,
                  ck_ref, cv_ref, o_ref, s_sc, a_sc, acc, qb, kbuf, vbuf, ksem, vsem, *, n_pg, lam_init):
    b = pl.program_id(0)
    ph = pl.program_id(1)
    g = pl.program_id(2)
    n_b = pl.num_programs(0)
    n_g = pl.num_programs(2)
    dq = DA_QK_DIM
    dv = DA_V_DIM
    nh = DA_HEADS
    n_pages = s_sc.shape[0] - 1
    lane = lax.broadcasted_iota(I32, qt_ref.shape, 1)

    def group_copies(cache_ref, buf, sem, bq, gq, slot):
        return [pltpu.make_async_copy(cache_ref.at[pt_ref[gq * n_pg + i, bq]], buf.at[slot, i], sem.at[slot])
                for i in range(n_pg)]

    def start_group(phq, bq, gq):
        slot = (bq * n_g + gq) % 2

        @pl.when(phq == 0)
        def _():
            for cp in group_copies(ck_ref, kbuf, ksem, bq, gq, slot):
                cp.start()

        @pl.when(phq == 1)
        def _():
            for cp in group_copies(cv_ref, vbuf, vsem, bq, gq, slot):
                cp.start()

    step = (b * 2 + ph) * n_g + g
    slot = (b * n_g + g) % 2

    @pl.when(step == 0)
    def _():
        start_group(ph, b, g)

    @pl.when(step + 1 < n_b * 2 * n_g)
    def _():
        nxt = step + 1
        start_group((nxt // n_g) % 2, nxt // (2 * n_g), nxt % n_g)

    k_refs = [kbuf.at[slot, i] for i in range(n_pg)]
    v_refs = [vbuf.at[slot, i] for i in range(n_pg)]

    def column(ref):
        return jnp.sum(jnp.where(lane == b, ref[...], 0.0), axis=1, keepdims=True)

    @pl.when((b == 0) & (ph == 0) & (g == 0))
    def _():
        o_ref[...] = jnp.zeros_like(o_ref)

    @pl.when((ph == 0) & (g == 0))
    def _():
        qb[...] = jnp.broadcast_to(column(qt_ref) * Q_SCALE, qb.shape)

    @pl.when(ph == 0)
    def _():
        for cp in group_copies(ck_ref, kbuf, ksem, b, g, slot):
            cp.wait()
        for c in range(2 * nh):
            qc = qb[c * dq:(c + 1) * dq, :]
            srow = (c % 2) * nh + c // 2
            for i in range(n_pg):
                prod = k_refs[i][c * dq:(c + 1) * dq, :] * qc
                s_sc[g * n_pg + i, srow:srow + 1, :] = jnp.sum(prod, axis=0, keepdims=True)

        @pl.when(g == n_g - 1)
        def _():
            s_sc[n_pages - 1] = s_sc[n_pages - 1] + bpage_ref[...]
            prodn = column(knt_ref) * qb[:, 0:1]
            sub = lax.broadcasted_iota(I32, (2 * nh, 1), 0)
            ln = bnew_ref[...]
            for r in range(2 * nh):
                c = (r % nh) * 2 + r // nh
                ln = ln + jnp.where(sub == r, jnp.sum(prodn[c * dq:(c + 1) * dq, :], axis=0, keepdims=True), 0.0)
            lane_s = lax.broadcasted_iota(I32, (2 * nh, LANES), 1)
            s_sc[n_pages] = jnp.where(lane_s == 0, ln, NEG)
            s_all = s_sc[...]
            mx = jnp.max(jnp.max(s_all, axis=0), axis=1, keepdims=True)
            p = jnp.exp(s_all - mx)
            l = jnp.sum(jnp.sum(p, axis=0), axis=1, keepdims=True)
            lam = _lam(lq1, lk1, lq2, lk2, lam_init)
            a_sc[...] = p[:, 0:nh, :] / l[0:nh, :] - lam * (p[:, nh:2 * nh, :] / l[nh:2 * nh, :])
            acc[...] = jnp.zeros_like(acc)

    @pl.when(ph == 1)
    def _():
        for cp in group_copies(cv_ref, vbuf, vsem, b, g, slot):
            cp.wait()
        for h in range(nh):
            part = acc[h * dv:(h + 1) * dv, :]
            for i in range(n_pg):
                part = part + v_refs[i][h * dv:(h + 1) * dv, :] * a_sc[g * n_pg + i, h:h + 1, :]
            acc[h * dv:(h + 1) * dv, :] = part

        @pl.when(g == n_g - 1)
        def _():
            out = jnp.sum(acc[...], axis=1, keepdims=True)
            vn = column(vnt_ref)
            a_new = a_sc[n_pages][:, 0:1]
            pieces = []
            for h in range(nh):
                oh = out[h * dv:(h + 1) * dv, :] + a_new[h:h + 1, :] * vn[h * dv:(h + 1) * dv, :]
                pieces.append(oh * lax.rsqrt(jnp.mean(oh * oh, axis=0, keepdims=True) + EPS))
            res = jnp.concatenate(pieces, axis=0) * dag_ref[...] * (1.0 - lam_init)
            o_ref[...] = jnp.where(lane == b, res, o_ref[...])


def _sample_attention(q_t, kn_t, vn_t, ck_t, cv_t, page_table_t, rel_bias, da_g, lams, lam_init):
    w, b = q_t.shape
    n_pages = page_table_t.shape[0]
    n_pg = min(PAGES_PER_STEP, n_pages)
    n_g = n_pages // n_pg
    ps = PAGE_SIZE
    past = n_pages * ps
    nh = DA_HEADS
    tab = _bias_by_distance(rel_bias, ps + 1)
    bpage = jnp.tile(tab[ps - jnp.arange(ps)].T, (2, 1))
    bnew = jnp.tile(tab[0:1].T, (2, 1))

    in_specs = [_const_spec((w, b))] * 3 + [_const_spec((2 * nh, ps)), _const_spec((2 * nh, 1)),
                                            _const_spec((w, 1))] + [_const_spec((1, DA_QK_DIM))] * 4
    in_specs += [pl.BlockSpec(memory_space=pl.ANY)] * 2
    grid_spec = pltpu.PrefetchScalarGridSpec(
        num_scalar_prefetch=1,
        grid=(b, 2, n_g),
        in_specs=in_specs,
        out_specs=_const_spec((w, b)),
        scratch_shapes=[pltpu.VMEM((n_pages + 1, 2 * nh, ps), F32), pltpu.VMEM((n_pages + 1, nh, ps), F32),
                        pltpu.VMEM((w, ps), F32), pltpu.VMEM((w, ps), F32),
                        pltpu.VMEM((2, n_pg, w, ps), F32), pltpu.VMEM((2, n_pg, w, ps), F32),
                        pltpu.SemaphoreType.DMA((2,)), pltpu.SemaphoreType.DMA((2,))],
    )
    return pl.pallas_call(
        functools.partial(_sattn_kernel, n_pg=n_pg, lam_init=lam_init),
        grid_spec=grid_spec,
        out_shape=jax.ShapeDtypeStruct((w, b), F32),
        compiler_params=_params(("arbitrary", "arbitrary", "arbitrary")),
        name="sample_attention",
    )(page_table_t, q_t, kn_t, vn_t, bpage, bnew, da_g.reshape(w, 1), *lams, ck_t, cv_t)


def _outproj_kernel(ap_ref, mop_ref, xp_ref, g1p_ref, sh2p_ref, sc2p_ref, g2p_ref,
                    as_ref, mos_ref, xs_ref, g1s_ref, sh2s_ref, sc2s_ref, g2s_ref,
                    wo_ref, n2_ref, wrt_ref, wsg_ref, wsu_ref, wsd_ref,
                    x1p_ref, x1s_ref, h2t_ref, lt_ref):
    i = pl.program_id(0)
    n_prompt = pl.num_programs(0) - 1

    def block(a, mo, x, g1, sh2, sc2, g2, out_ref):
        half = a.shape[1]
        mix = jnp.dot(a, wo_ref[0:half, :], preferred_element_type=F32)
        mix = mix + jnp.dot(mo, wo_ref[half:2 * half, :], preferred_element_type=F32)
        x1 = x + g1 * mix
        h2 = _modulated_norm(x1, n2_ref[...], sh2, sc2)
        lt_ref[...] = lax.dot_general(wrt_ref[...], h2, NT, precision=HI, preferred_element_type=F32)
        tm = h2.shape[0]
        for s in range(h2.shape[1] // LANES):
            h2t_ref[pl.ds(s, tm, stride=SUBLANES), :] = h2[:, s * LANES:(s + 1) * LANES]
        hb = h2.astype(BF16)
        gate = jnp.dot(hb, wsg_ref[...], preferred_element_type=F32)
        up = jnp.dot(hb, wsu_ref[...], preferred_element_type=F32)
        act = (gate * _sigmoid(gate) * up).astype(BF16)
        shared = jnp.dot(act, wsd_ref[...], preferred_element_type=F32)
        out_ref[...] = x1 + g2 * shared

    @pl.when(i < n_prompt)
    def _():
        block(ap_ref[...], mop_ref[...], xp_ref[...], g1p_ref[...], sh2p_ref[...], sc2p_ref[...], g2p_ref[...],
              x1p_ref)

    @pl.when(i == n_prompt)
    def _():
        block(as_ref[...].T.astype(BF16), mos_ref[...].T.astype(BF16), xs_ref[...], g1s_ref[...], sh2s_ref[...],
              sc2s_ref[...], g2s_ref[...], x1s_ref)


def _outproj(a_p, mo_p, x_p, mod_p, a_ts, mo_ts, x_s, mod_s, w_out, n2, w_router_t, wsg, wsu, wsd):
    t, d = x_p.shape
    bs = x_s.shape[0]
    tm = TOK_BLK
    assert bs == tm and t % tm == 0
    nbp = t // tm
    e = w_router_t.shape[0]
    nsl = d // LANES
    prow = lambda n: pl.BlockSpec((tm, n), lambda i: (jnp.minimum(i, nbp - 1), 0))
    in_specs = ([prow(DA_COLS), prow(ML_COLS), prow(d)] + [_const_spec((1, d))] * 4
                + [_const_spec((DA_COLS, bs)), _const_spec((ML_COLS, bs)), _const_spec((bs, d))]
                + [_const_spec((bs, d))] * 4
                + [_const_spec(w_out.shape), _const_spec((1, d)), _const_spec(w_router_t.shape),
                   _const_spec(wsg.shape), _const_spec(wsu.shape), _const_spec(wsd.shape)])
    return pl.pallas_call(
        _outproj_kernel,
        grid=(nbp + 1,),
        in_specs=in_specs,
        out_specs=[prow(d), _const_spec((bs, d)), pl.BlockSpec((tm * nsl, LANES), lambda i: (i, 0)),
                   pl.BlockSpec((e, tm), lambda i: (0, i))],
        out_shape=[jax.ShapeDtypeStruct((t, d), F32), jax.ShapeDtypeStruct((bs, d), F32),
                   jax.ShapeDtypeStruct(((t + bs) * nsl, LANES), F32), jax.ShapeDtypeStruct((e, t + bs), F32)],
        compiler_params=_params(("arbitrary",)),
        name="outproj",
    )(a_p, mo_p, x_p, *mod_p, a_ts, mo_ts, x_s, *mod_s, w_out, n2, w_router_t, wsg, wsu, wsd)


def _first_max(cur, iota, n, axis=0):
    t = jnp.max(cur, axis=axis, keepdims=True)
    idx = jnp.min(jnp.where(cur == t, iota, float(n)), axis=axis, keepdims=True)
    return t, idx


def _route_kernel(lt_ref, bias_ref, ut_ref, ones_ref, eidx_ref, w_ref, rank_ref, cnt_ref, carry):
    i = pl.program_id(0)

    @pl.when(i == 0)
    def _():
        carry[...] = jnp.zeros_like(carry)

    aff = _sigmoid(lt_ref[...])
    sel = aff + bias_ref[...]
    n_e, n_t = aff.shape
    gsz = n_e // N_GROUPS
    e_iota = lax.broadcasted_iota(I32, (n_e, n_t), 0).astype(F32)
    g_iota = lax.broadcasted_iota(I32, (N_GROUPS, n_t), 0).astype(F32)
    in_iota = lax.broadcasted_iota(I32, (gsz, n_t), 0).astype(F32)
    ninf = -jnp.inf
    gscore = jnp.zeros((N_GROUPS, n_t), F32)
    for g in range(N_GROUPS):
        blk = sel[g * gsz:(g + 1) * gsz, :]
        t1, i1 = _first_max(blk, in_iota, gsz)
        t2 = jnp.max(jnp.where(in_iota == i1, ninf, blk), axis=0, keepdims=True)
        gscore = jnp.where(g_iota == float(g), t1 + t2, gscore)
    gmask = jnp.zeros((N_GROUPS, n_t), F32)
    cur = gscore
    for _ in range(TOPK_GROUPS):
        _, gi = _first_max(cur, g_iota, N_GROUPS)
        hit = g_iota == gi
        gmask = jnp.where(hit, 1.0, gmask)
        cur = jnp.where(hit, ninf, cur)
    cur = jnp.concatenate(
        [jnp.where(gmask[g:g + 1, :] > 0.0, sel[g * gsz:(g + 1) * gsz, :], ninf) for g in range(N_GROUPS)], axis=0)
    hits, ws = [], []
    msum = jnp.zeros((n_e, n_t), F32)
    for k in range(TOP_K):
        _, ei = _first_max(cur, e_iota, n_e)
        hit = e_iota == ei
        ws.append(jnp.sum(jnp.where(hit, aff, 0.0), axis=0, keepdims=True))
        cur = jnp.where(hit, ninf, cur)
        msum = msum + hit.astype(F32)
        hits.append(hit)
        eidx_ref[k:k + 1, :] = ei.astype(I32)
    wsum = ws[0]
    for k in range(1, TOP_K):
        wsum = wsum + ws[k]
    mb = msum.astype(BF16)
    base = carry[...] + jnp.dot(mb, ut_ref[...], preferred_element_type=F32)
    for k in range(TOP_K):
        w_ref[k:k + 1, :] = ws[k] / wsum * ROUTED_SCALE
        rank_ref[k:k + 1, :] = jnp.sum(jnp.where(hits[k], base, 0.0), axis=0, keepdims=True).astype(I32)
    carry[...] = carry[...] + jnp.dot(mb, ones_ref[...], preferred_element_type=F32)
    cnt_ref[...] = carry[...]


def _route(logits_t, router_bias):
    n_e, t_all = logits_t.shape
    tb = TOK_BLK
    ut = jnp.asarray(np.triu(np.ones((tb, tb), np.float32), 1), dtype=BF16)
    ones = jnp.ones((tb, tb), BF16)
    bias_b = jnp.broadcast_to(router_bias.astype(F32)[:, None], (n_e, tb))
    tok = pl.BlockSpec((TOP_K, tb), lambda i: (0, i))
    return pl.pallas_call(
        _route_kernel,
        grid=(t_all // tb,),
        in_specs=[pl.BlockSpec((n_e, tb), lambda i: (0, i)), _const_spec((n_e, tb)), _const_spec((tb, tb)),
                  _const_spec((tb, tb))],
        out_specs=[tok, tok, tok, _const_spec((n_e, tb))],
        out_shape=[jax.ShapeDtypeStruct((TOP_K, t_all), I32), jax.ShapeDtypeStruct((TOP_K, t_all), F32),
                   jax.ShapeDtypeStruct((TOP_K, t_all), I32), jax.ShapeDtypeStruct((n_e, tb), F32)],
        scratch_shapes=[pltpu.VMEM((n_e, tb), F32)],
        compiler_params=_params(("arbitrary",)),
        name="moe_route",
    )(logits_t, bias_b, ut, ones)


def _pos_kernel(eidx_ref, rank_ref, offs_ref, pos_ref):
    offs = offs_ref[...]
    e_iota = lax.broadcasted_iota(I32, offs.shape, 0)
    for k in range(TOP_K):
        hit = e_iota == eidx_ref[k:k + 1, :]
        base = jnp.sum(jnp.where(hit, offs, 0.0), axis=0, keepdims=True).astype(I32)
        pos_ref[k:k + 1, :] = base + rank_ref[k:k + 1, :]


def _positions(eidx, rank, offs):
    t_all = eidx.shape[1]
    tb = TOK_BLK
    n_e = offs.shape[0]
    tok = pl.BlockSpec((TOP_K, tb), lambda i: (0, i))
    return pl.pallas_call(
        _pos_kernel,
        grid=(t_all // tb,),
        in_specs=[tok, tok, _const_spec((n_e, tb))],
        out_specs=tok,
        out_shape=jax.ShapeDtypeStruct((TOP_K, t_all), I32),
        compiler_params=_params(("parallel",)),
        name="moe_positions",
    )(eidx, rank, jnp.broadcast_to(offs.astype(F32)[:, None], (n_e, tb)))


def _tile_start(row):
    start = row * SUBLANES
    return start if isinstance(start, int) else pl.multiple_of(start, SUBLANES)


def _row_copy(src, src_row, dst, dst_row, sem):
    return pltpu.make_async_copy(src.at[pl.ds(_tile_start(src_row), SUBLANES)],
                                 dst.at[pl.ds(_tile_start(dst_row), SUBLANES)], sem)


def _dispatch_kernel(pos_ref, h2t_ref, xs_in_ref, xs_ref, sem):
    del xs_in_ref
    tb = pos_ref.shape[1]

    def issue(t, carry):
        for k in range(TOP_K):
            _row_copy(h2t_ref, t, xs_ref, pos_ref[k, t], sem).start()
        return carry

    lax.fori_loop(0, tb, issue, 0)

    def drain(t, carry):
        for k in range(TOP_K):
            _row_copy(h2t_ref, 0, xs_ref, 0, sem).wait()
        return carry

    lax.fori_loop(0, tb, drain, 0)


def _dispatch(pos, h2t, n_rows):
    t_all = pos.shape[1]
    tb = TOK_BLK
    xs0 = jnp.zeros((n_rows * SUBLANES, LANES), F32)
    return pl.pallas_call(
        _dispatch_kernel,
        grid=(t_all // tb,),
        in_specs=[pl.BlockSpec((TOP_K, tb), lambda i: (0, i), memory_space=pltpu.SMEM),
                  pl.BlockSpec((tb * SUBLANES, LANES), lambda i: (i, 0)), pl.BlockSpec(memory_space=pl.ANY)],
        out_specs=pl.BlockSpec(memory_space=pl.ANY),
        out_shape=jax.ShapeDtypeStruct(xs0.shape, F32),
        input_output_aliases={2: 0},
        scratch_shapes=[pltpu.SemaphoreType.DMA(())],
        compiler_params=_params(("arbitrary",)),
        name="moe_dispatch",
    )(pos, h2t, xs0)


def _experts_kernel(be_ref, nv_ref, xs_ref, wg_ref, wu_ref, wd_ref, ys_ref, wgb, wub, wdb):
    i = pl.program_id(0)

    @pl.when(i < nv_ref[0])
    def _():
        @pl.when((i == 0) | (be_ref[i] != be_ref[jnp.maximum(i - 1, 0)]))
        def _():
            wgb[...] = wg_ref[...].astype(BF16)
            wub[...] = wu_ref[...].astype(BF16)
            wdb[...] = wd_ref[...].astype(BF16)

        bm = xs_ref.shape[0] // SUBLANES
        nsl = wg_ref.shape[0] // LANES
        x = jnp.concatenate([xs_ref[pl.ds(s, bm, stride=SUBLANES), :] for s in range(nsl)], axis=1).astype(BF16)
        gate = jnp.dot(x, wgb[...], preferred_element_type=F32)
        up = jnp.dot(x, wub[...], preferred_element_type=F32)
        act = (gate * _sigmoid(gate) * up).astype(BF16)
        y = jnp.dot(act, wdb[...], preferred_element_type=F32)
        for s in range(nsl):
            ys_ref[pl.ds(s, bm, stride=SUBLANES), :] = y[:, s * LANES:(s + 1) * LANES]


def _experts(block_e, n_valid, xs, w_g, w_u, w_d, n_blocks):
    bm = MOE_BM
    d, de = w_g.shape[1], w_g.shape[2]
    rows = pl.BlockSpec((bm * SUBLANES, LANES), lambda i, be, nv: (jnp.minimum(i, nv[0] - 1), 0))
    grid_spec = pltpu.PrefetchScalarGridSpec(
        num_scalar_prefetch=2,
        grid=(n_blocks,),
        in_specs=[rows,
                  pl.BlockSpec((None, d, de), lambda i, be, nv: (be[i], 0, 0)),
                  pl.BlockSpec((None, d, de), lambda i, be, nv: (be[i], 0, 0)),
                  pl.BlockSpec((None, de, d), lambda i, be, nv: (be[i], 0, 0))],
        out_specs=rows,
        scratch_shapes=[pltpu.VMEM((d, de), BF16), pltpu.VMEM((d, de), BF16), pltpu.VMEM((de, d), BF16)],
    )
    return pl.pallas_call(
        _experts_kernel,
        grid_spec=grid_spec,
        out_shape=jax.ShapeDtypeStruct(xs.shape, F32),
        input_output_aliases={2: 0},
        compiler_params=_params(("arbitrary",)),
        name="moe_experts",
    )(block_e, n_valid, xs, w_g, w_u, w_d)


def _combine_kernel(pos_ref, w_ref, x1p_ref, g2p_ref, x1s_ref, g2s_ref, fg_ref, ys_ref, yp_ref, ysm_ref, buf, sem):
    i = pl.program_id(0)
    n_prompt = pl.num_programs(0) - 1
    tb = pos_ref.shape[1]
    nsl = x1p_ref.shape[1] // LANES

    def issue(t, carry):
        for k in range(TOP_K):
            pltpu.make_async_copy(ys_ref.at[pl.ds(_tile_start(pos_ref[k, t]), SUBLANES)],
                                  buf.at[k, pl.ds(_tile_start(t), SUBLANES)], sem).start()
        return carry

    lax.fori_loop(0, tb, issue, 0)

    def drain(t, carry):
        for k in range(TOP_K):
            pltpu.make_async_copy(ys_ref.at[pl.ds(0, SUBLANES)], buf.at[k, pl.ds(0, SUBLANES)], sem).wait()
        return carry

    lax.fori_loop(0, tb, drain, 0)

    w = w_ref[...]
    routed = None
    for k in range(TOP_K):
        yk = jnp.concatenate([buf[k, pl.ds(s, tb, stride=SUBLANES), :] for s in range(nsl)], axis=1)
        term = yk * w[:, k:k + 1]
        routed = term if routed is None else routed + term

    def finish(x1, g2, out_ref):
        x2 = x1 + g2 * routed
        out_ref[...] = x2 * lax.rsqrt(jnp.mean(x2 * x2, axis=-1, keepdims=True) + EPS) * fg_ref[...]

    @pl.when(i < n_prompt)
    def _():
        finish(x1p_ref[...], g2p_ref[...], yp_ref)

    @pl.when(i == n_prompt)
    def _():
        finish(x1s_ref[...], g2s_ref[...], ysm_ref)


def _combine(pos, w_tm, x1_p, g2_p, x1_s, g2_s, final_g, ys):
    t, d = x1_p.shape
    bs = x1_s.shape[0]
    tb = TOK_BLK
    assert bs == tb and t % tb == 0
    nbp = t // tb
    prow = pl.BlockSpec((tb, d), lambda i: (jnp.minimum(i, nbp - 1), 0))
    return pl.pallas_call(
        _combine_kernel,
        grid=(nbp + 1,),
        in_specs=[pl.BlockSpec((TOP_K, tb), lambda i: (0, i), memory_space=pltpu.SMEM),
                  pl.BlockSpec((tb, TOP_K), lambda i: (i, 0)),
                  prow, _const_spec((1, d)), _const_spec((bs, d)), _const_spec((bs, d)), _const_spec((1, d)),
                  pl.BlockSpec(memory_space=pl.ANY)],
        out_specs=[prow, _const_spec((bs, d))],
        out_shape=[jax.ShapeDtypeStruct((t, d), F32), jax.ShapeDtypeStruct((bs, d), F32)],
        scratch_shapes=[pltpu.VMEM((TOP_K, tb * SUBLANES, LANES), F32), pltpu.SemaphoreType.DMA(())],
        compiler_params=_params(("arbitrary",)),
        name="moe_combine",
    )(pos, w_tm, x1_p, g2_p, x1_s, g2_s, final_g.reshape(1, d), ys)


def kernel(x_prompt, x_sample, cache_k, cache_v, state_C, state_n, state_m, state_conv, page_table, c_prompt, c_sample, w_ada, b_ada, norm1_g, norm2_g, w_in, conv_w, conv_b, lambda_q1, lambda_k1, lambda_q2, lambda_k2, da_norm_g, rel_bias, ml_b_i, ml_b_f, ml_norm_g, w_out, w_router, router_bias, w_exp_gate, w_exp_up, w_exp_down, w_sh_gate, w_sh_up, w_sh_down, final_norm_g):
    bp, t, d = x_prompt.shape
    bs, ts, _ = x_sample.shape
    depth = w_ada.shape[0]
    assert bp == 1 and ts == 1 and depth == 1
    assert d % LANES == 0 and (t + bs) % TOK_BLK == 0 and t % TOK_BLK == 0
    l = 0
    lam_init = 0.8 - 0.6 * math.exp(-0.3 * l)
    lams = [a[l].reshape(1, -1).astype(F32) for a in (lambda_q1, lambda_k1, lambda_q2, lambda_k2)]
    t_all = t + bs
    hd = ML_HEAD_DIM

    pad = (-(bp + bs)) % SUBLANES
    c_all = jnp.concatenate([c_prompt, c_sample, jnp.zeros((pad, d), F32)], axis=0)
    mod = _adaln(c_all, w_ada[l], b_ada[l])
    modp = [mod[0:1, i * d:(i + 1) * d] for i in range(6)]
    mods = [mod[1:1 + bs, i * d:(i + 1) * d] for i in range(6)]

    w_in_t = w_in[l].T
    c = DA_COLS
    wqkv_t = w_in_t[0:3 * c].astype(BF16)
    wml_t = w_in_t[3 * c:3 * c + 4 * ML_COLS].astype(BF16)
    wg_t = w_in_t[3 * c + 4 * ML_COLS:]
    b_gate = jnp.concatenate([ml_b_i[l], ml_b_f[l]]).astype(F32)
    w_out_b = w_out[l].astype(BF16)
    w_router_t = w_router[l].T
    wsg, wsu, wsd = w_sh_gate[l].astype(BF16), w_sh_up[l].astype(BF16), w_sh_down[l].astype(BF16)
    n1 = norm1_g[l].reshape(1, d)
    n2 = norm2_g[l].reshape(1, d)

    xp = x_prompt.reshape(t, d)
    (qbf, ktf, ktb, vtf, vtb, mqk, mv, mo_raw, gates, gates_t) = _inproj(
        xp, modp[0], modp[1], n1, wqkv_t[0:c], wqkv_t[c:3 * c], wml_t, wg_t, b_gate)
    attn_p = _prompt_attention(qbf, ktb, vtb, rel_bias, da_norm_g[l], lams, lam_init)
    mo_p, s_p, m_p = _mlstm_prompt(mqk, mv, mo_raw, gates, gates_t, conv_w[l], conv_b[l], ml_norm_g[l])

    xs = x_sample.reshape(bs, d)
    qkvt_s, mqk_s, mvot_s, gates_ts = _sample_inproj(xs, mods[0], mods[1], n1, wqkv_t, wml_t, wg_t, b_gate)
    conv_buf = jnp.transpose(state_conv[l], (1, 0, 2))
    qkt_s, conv_new_s = _sample_conv(mqk_s, conv_buf, conv_w[l], conv_b[l])
    c_t = jnp.transpose(state_C[l], (1, 2, 3, 0))
    n_t = jnp.transpose(state_n[l], (1, 2, 0))
    m_t = state_m[l].T
    c_s, n_s, m_s, mo_ts = _sample_mlstm(qkt_s, mvot_s[0:ML_COLS], mvot_s[ML_COLS:2 * ML_COLS], gates_ts, m_t, c_t,
                                         n_t, ml_norm_g[l])
    n_pool = cache_k.shape[1]
    ck_t = jnp.transpose(cache_k[l], (0, 2, 3, 1)).reshape(n_pool, c, PAGE_SIZE)
    cv_t = jnp.transpose(cache_v[l], (0, 2, 3, 1)).reshape(n_pool, c, PAGE_SIZE)
    attn_ts = _sample_attention(qkvt_s[0:c], qkvt_s[c:2 * c], qkvt_s[2 * c:3 * c], ck_t, cv_t,
                                page_table.T, rel_bias, da_norm_g[l], lams, lam_init)

    xs1_p, xs1_s, h2t, logits_t = _outproj(attn_p, mo_p, xp, modp[2:6], attn_ts, mo_ts, xs, mods[2:6], w_out_b, n2,
                                           w_router_t, wsg, wsu, wsd)

    n_e = w_router.shape[2]
    eidx, wts, rank, cnt = _route(logits_t, router_bias[l])
    counts = cnt[:, 0].astype(I32)
    padded = (counts + MOE_BM - 1) // MOE_BM * MOE_BM
    pad_end = jnp.cumsum(padded)
    offs = pad_end - padded
    n_blocks = -(-(t_all * TOP_K) // MOE_BM) + n_e
    block_e = jnp.minimum(jnp.searchsorted(pad_end, jnp.arange(n_blocks) * MOE_BM, side='right'),
                          n_e - 1).astype(I32)
    n_valid = (pad_end[-1:] // MOE_BM).astype(I32)
    pos = _positions(eidx, rank, offs)
    xs_sorted = _dispatch(pos, h2t, n_blocks * MOE_BM)
    ys = _experts(block_e, n_valid, xs_sorted, w_exp_gate[l], w_exp_up[l], w_exp_down[l], n_blocks)
    w_tm = wts.T
    y_p, y_s = _combine(pos, w_tm, xs1_p, modp[5], xs1_s, mods[5], final_norm_g, ys)

    def per_head(a_t, n_feat):
        return jnp.transpose(a_t.reshape(-1, n_feat, a_t.shape[1]), (2, 0, 1))

    return (
        y_p.reshape(bp, t, d),
        y_s.reshape(bs, ts, d),
        per_head(ktf, 2 * DA_QK_DIM).reshape(1, bp, t, DA_HEADS, 2 * DA_QK_DIM),
        per_head(vtf, DA_V_DIM).reshape(1, bp, t, DA_HEADS, DA_V_DIM),
        s_p[:, :, :hd].reshape(1, bp, ML_HEADS, hd, hd),
        s_p[:, :, hd].reshape(1, bp, ML_HEADS, hd),
        m_p.reshape(1, bp, ML_HEADS),
        mqk[t - (CONV_W - 1):, :].reshape(1, bp, CONV_W - 1, 2 * ML_COLS),
        per_head(qkvt_s[c:2 * c], 2 * DA_QK_DIM).reshape(1, bs, ts, DA_HEADS, 2 * DA_QK_DIM),
        per_head(qkvt_s[2 * c:3 * c], DA_V_DIM).reshape(1, bs, ts, DA_HEADS, DA_V_DIM),
        jnp.transpose(c_s, (3, 0, 1, 2)).reshape(1, bs, ML_HEADS, hd, hd),
        jnp.transpose(n_s, (2, 0, 1)).reshape(1, bs, ML_HEADS, hd),
        m_s.reshape(ML_HEADS, bs).T.reshape(1, bs, ML_HEADS),
        jnp.transpose(conv_new_s, (1, 0, 2)).reshape(1, bs, CONV_W - 1, 2 * ML_COLS),
    )
```

```python
import functools
import math

import jax
import jax.numpy as jnp
import numpy as np
from jax import lax
from jax.experimental import pallas as pl
from jax.experimental.pallas import tpu as pltpu

F32 = jnp.float32
BF16 = jnp.bfloat16
I32 = jnp.int32
HI = lax.Precision.HIGHEST

DA_HEADS = 8
DA_QK_DIM = 32
DA_V_DIM = 64
ML_HEADS = 8
ML_HEAD_DIM = 64
CONV_W = 4
NUM_BUCKETS = 32
MAX_DISTANCE = 128
PAGE_SIZE = 128
N_GROUPS = 8
TOPK_GROUPS = 4
TOP_K = 8
ROUTED_SCALE = 2.5
EPS = 1e-6

DA_COLS = DA_HEADS * 2 * DA_QK_DIM
ML_COLS = ML_HEADS * ML_HEAD_DIM
N_GATES = 2 * ML_HEADS
Q_SCALE = DA_QK_DIM ** -0.5
K_SCALE = ML_HEAD_DIM ** -0.5
LOG2E = 1.0 / math.log(2.0)
NEG = -1e30

LANES = 128
SUBLANES = 8
VMEM_LIMIT = 56 * 1024 * 1024

ML_CHUNK = 128
ATT_TILE = 512
ATT_SUB = 512
MOE_BM = 256
TOK_BLK = 128
PAGES_PER_STEP = 16
PROJ_TM = 256

NT = (((1,), (1,)), ((), ()))


def _const_spec(shape):
    nd = len(shape)
    return pl.BlockSpec(shape, lambda *_: (0,) * nd)


def _params(sem, vmem=VMEM_LIMIT):
    return pltpu.CompilerParams(dimension_semantics=sem, vmem_limit_bytes=vmem)


def _sigmoid(x):
    return 1.0 / (1.0 + jnp.exp(-x))


def _log_sigmoid(x):
    return jnp.minimum(x, 0.0) - jnp.log1p(jnp.exp(-jnp.abs(x)))


def _lam(lq1, lk1, lq2, lk2, lam_init):
    a = jnp.sum(lq1[...] * lk1[...], axis=-1, keepdims=True)
    b = jnp.sum(lq2[...] * lk2[...], axis=-1, keepdims=True)
    return jnp.exp(a) - jnp.exp(b) + lam_init


def _modulated_norm(x, g, shift, scale):
    y = x * lax.rsqrt(jnp.mean(x * x, axis=-1, keepdims=True) + EPS) * g
    return y * (1.0 + scale) + shift


def _t5_bucket(dist):
    n = jnp.maximum(dist, 0)
    max_exact = NUM_BUCKETS // 2
    nf = jnp.maximum(n, 1).astype(F32)
    large = max_exact + (jnp.log(nf / max_exact) / math.log(MAX_DISTANCE / max_exact)
                         * (NUM_BUCKETS - max_exact)).astype(I32)
    return jnp.where(n < max_exact, n, jnp.minimum(large, NUM_BUCKETS - 1))


def _bias_by_distance(rel_bias, n):
    rb = rel_bias.astype(F32)
    return rb[_t5_bucket(jnp.arange(n))] - rb[NUM_BUCKETS - 1][None, :]


def _adaln_kernel(c_ref, w_ref, b_ref, o_ref):
    c = c_ref[...]
    a = c * _sigmoid(c)
    o_ref[...] = jnp.dot(a, w_ref[...], precision=HI, preferred_element_type=F32) + b_ref[...]


def _adaln(c_all, w, b):
    r, d = c_all.shape
    n = w.shape[1]
    tn = n // 4
    return pl.pallas_call(
        _adaln_kernel,
        grid=(n // tn,),
        in_specs=[_const_spec((r, d)), pl.BlockSpec((d, tn), lambda j: (0, j)),
                  pl.BlockSpec((1, tn), lambda j: (0, j))],
        out_specs=pl.BlockSpec((r, tn), lambda j: (0, j)),
        out_shape=jax.ShapeDtypeStruct((r, n), F32),
        compiler_params=_params(("parallel",)),
        name="adaln",
    )(c_all, w, b.reshape(1, n))


def _inproj_kernel(x_ref, sh_ref, sc_ref, g_ref, wqkv_ref, wml_ref, wg_ref, bg_ref, bgt_ref,
                   qtb_ref, krb_ref, ktf_ref, vtf_ref, vtb_ref, mqk_ref, mv_ref, mo_ref, gates_ref, gatest_ref):
    h = _modulated_norm(x_ref[...], g_ref[...], sh_ref[...], sc_ref[...])
    hb = h.astype(BF16)
    c = DA_COLS
    qkvt = lax.dot_general(wqkv_ref[...], hb, NT, preferred_element_type=F32)
    qtb_ref[...] = (qkvt[0:c, :] * (Q_SCALE * LOG2E)).astype(BF16)
    ktf_ref[...] = qkvt[c:2 * c, :]
    vtf_ref[...] = qkvt[2 * c:3 * c, :]
    vtb_ref[...] = qkvt[2 * c:3 * c, :].astype(BF16)
    krb_ref[...] = lax.dot_general(hb, wqkv_ref[c:2 * c, :], NT, preferred_element_type=F32).astype(BF16)
    ml = lax.dot_general(hb, wml_ref[...], NT, preferred_element_type=F32)
    mqk_ref[...] = ml[:, 0:2 * ML_COLS]
    mv_ref[...] = ml[:, 2 * ML_COLS:3 * ML_COLS].astype(BF16)
    mo_ref[...] = ml[:, 3 * ML_COLS:4 * ML_COLS]
    graw = lax.dot_general(h, wg_ref[...], NT, precision=HI, preferred_element_type=F32) + bg_ref[...]
    lane = lax.broadcasted_iota(I32, graw.shape, 1)
    gates_ref[...] = jnp.where(lane < ML_HEADS, graw, _log_sigmoid(graw))
    grawt = lax.dot_general(wg_ref[...], h, NT, precision=HI, preferred_element_type=F32) + bgt_ref[...]
    sub = lax.broadcasted_iota(I32, grawt.shape, 0)
    gatest_ref[...] = jnp.where(sub < ML_HEADS, grawt, _log_sigmoid(grawt))


def _inproj(x, shift, scale, g, wqkv_t, wml_t, wg_t, b_gate):
    r, d = x.shape
    tm = min(PROJ_TM, r)
    row = lambda n: pl.BlockSpec((tm, n), lambda i: (i, 0))
    col = lambda n: pl.BlockSpec((n, tm), lambda i: (0, i))
    ng = N_GATES
    out_shape = [
        jax.ShapeDtypeStruct((DA_COLS, r), BF16), jax.ShapeDtypeStruct((r, DA_COLS), BF16),
        jax.ShapeDtypeStruct((DA_COLS, r), F32),
        jax.ShapeDtypeStruct((DA_COLS, r), F32), jax.ShapeDtypeStruct((DA_COLS, r), BF16),
        jax.ShapeDtypeStruct((r, 2 * ML_COLS), F32), jax.ShapeDtypeStruct((r, ML_COLS), BF16),
        jax.ShapeDtypeStruct((r, ML_COLS), F32),
        jax.ShapeDtypeStruct((r, ng), F32), jax.ShapeDtypeStruct((ng, r), F32),
    ]
    out_specs = [col(DA_COLS), row(DA_COLS), col(DA_COLS), col(DA_COLS), col(DA_COLS), row(2 * ML_COLS),
                 row(ML_COLS), row(ML_COLS), row(ng), col(ng)]
    return pl.pallas_call(
        _inproj_kernel,
        grid=(r // tm,),
        in_specs=[row(d), _const_spec((1, d)), _const_spec((1, d)), _const_spec((1, d)),
                  _const_spec(wqkv_t.shape), _const_spec(wml_t.shape),
                  _const_spec(wg_t.shape), _const_spec((1, ng)), _const_spec((ng, 1))],
        out_specs=out_specs,
        out_shape=out_shape,
        compiler_params=_params(("parallel",)),
        name="inproj",
    )(x, shift, scale, g, wqkv_t, wml_t, wg_t, b_gate.reshape(1, ng), b_gate.reshape(ng, 1))


def _pair_rmsnorm(x, lane, nd):
    sq = x * x
    lo = jnp.sum(jnp.where(lane < nd, sq, 0.0), axis=-1, keepdims=True) * (1.0 / nd)
    hi = jnp.sum(jnp.where(lane >= nd, sq, 0.0), axis=-1, keepdims=True) * (1.0 / nd)
    return x * jnp.where(lane < nd, lax.rsqrt(lo + EPS), lax.rsqrt(hi + EPS))


def _mlstm_kernel(mqk_ref, mv_ref, og_ref, gates_ref, gatest_ref, cw_ref, cb_ref, mlg_ref,
                  tri_ref, trit_ref, mo_ref, sout_ref, mout_ref, xprev, st, msc):
    c = pl.program_id(0)
    L = mqk_ref.shape[0]
    hd = ML_HEAD_DIM

    @pl.when(c == 0)
    def _():
        xprev[...] = jnp.zeros_like(xprev)
        st[...] = jnp.zeros_like(st)
        msc[...] = jnp.zeros_like(msc)

    x = mqk_ref[...]
    xp = xprev[...]
    row = lax.broadcasted_iota(I32, x.shape, 0)

    def shifted(j):
        return jnp.where(row < j, pltpu.roll(xp, j, 0), pltpu.roll(x, j, 0))

    y = cb_ref[...] + shifted(3) * cw_ref[0:1, :]
    y = y + shifted(2) * cw_ref[1:2, :]
    y = y + shifted(1) * cw_ref[2:3, :]
    y = y + x * cw_ref[3:4, :]
    qk = y * _sigmoid(y)
    xprev[...] = x

    g = gates_ref[...]
    gt = gatest_ref[...]
    fcol_all = jnp.dot(tri_ref[...], g, precision=HI, preferred_element_type=F32)
    frow_all = jnp.dot(gt, trit_ref[...], precision=HI, preferred_element_type=F32)
    r_i = lax.broadcasted_iota(I32, (L, L), 0)
    c_i = lax.broadcasted_iota(I32, (L, L), 1)
    causal = c_i <= r_i
    lane = lax.broadcasted_iota(I32, (L, 2 * hd), 1)

    for p in range(ML_HEADS // 2):
        qpair = qk[:, 2 * hd * p:2 * hd * (p + 1)]
        kpair = qk[:, ML_COLS + 2 * hd * p:ML_COLS + 2 * hd * (p + 1)] * K_SCALE
        vpair = mv_ref[:, 2 * hd * p:2 * hd * (p + 1)].astype(F32)
        hh, decays, vaugs, carries = [], [], [], []
        for j in range(2):
            h = 2 * p + j
            q_h = qpair[:, hd * j:hd * (j + 1)].astype(BF16)
            k_h = kpair[:, hd * j:hd * (j + 1)].astype(BF16)
            s_raw = lax.dot_general(q_h, k_h, NT, preferred_element_type=F32)
            fc = fcol_all[:, ML_HEADS + h:ML_HEADS + h + 1]
            fr = frow_all[ML_HEADS + h:ML_HEADS + h + 1, :]
            igr = gt[h:h + 1, :]
            igc = g[:, h:h + 1]
            m_old = msc[h]
            dlog = jnp.where(causal, fc - fr + igr, NEG)
            inter = m_old + fc
            m_t = jnp.maximum(inter, jnp.max(dlog, axis=-1, keepdims=True))
            dmat = jnp.where(causal, jnp.exp(dlog - m_t), 0.0)
            smat = (s_raw * dmat).astype(BF16)
            vsrc = vpair if j == 0 else pltpu.roll(vpair, hd, 1)
            vaug = jnp.where(lane < hd, vsrc, jnp.where(lane == hd, 1.0, 0.0)).astype(BF16)
            s_h = st[h]
            q_s = jnp.dot(q_h, s_h.astype(BF16), preferred_element_type=F32)
            num = jnp.dot(smat, vaug, preferred_element_type=F32) + jnp.exp(inter - m_t) * q_s
            den = num[:, hd:hd + 1]
            hh.append(num / jnp.maximum(jnp.abs(den), jnp.exp(-m_t)))
            m_new = m_t[L - 1:L, :]
            f_last = fc[L - 1:L, :]
            decays.append(jnp.exp(f_last - fc + igc - m_new))
            carries.append(jnp.exp(m_old + f_last - m_new))
            vaugs.append(vaug)
            msc[h] = m_new
        kd = kpair * jnp.where(lane < hd, decays[0], decays[1])
        kdt = kd.T
        for j in range(2):
            h = 2 * p + j
            upd = jnp.dot(kdt[hd * j:hd * (j + 1), :].astype(BF16), vaugs[j], preferred_element_type=F32)
            st[h] = carries[j] * st[h] + upd
        hpair = jnp.where(lane < hd, hh[0], pltpu.roll(hh[1], hd, 1))
        hn = _pair_rmsnorm(hpair, lane, hd) * mlg_ref[:, 2 * hd * p:2 * hd * (p + 1)]
        og = _sigmoid(og_ref[:, 2 * hd * p:2 * hd * (p + 1)])
        mo_ref[:, 2 * hd * p:2 * hd * (p + 1)] = (hn * og).astype(BF16)

    sout_ref[...] = st[...]
    mout_ref[...] = msc[...]


def _mlstm_prompt(mqk, mv, og, gates, gates_t, conv_w, conv_b, ml_g):
    t = mqk.shape[0]
    L = min(ML_CHUNK, t)
    hd = ML_HEAD_DIM
    tri = jnp.asarray(np.tril(np.ones((L, L), np.float32)))
    row = lambda n: pl.BlockSpec((L, n), lambda i: (i, 0))
    ng = N_GATES
    return pl.pallas_call(
        _mlstm_kernel,
        grid=(t // L,),
        in_specs=[row(2 * ML_COLS), row(ML_COLS), row(ML_COLS), row(ng),
                  pl.BlockSpec((ng, L), lambda i: (0, i)),
                  _const_spec((CONV_W, 2 * ML_COLS)), _const_spec((1, 2 * ML_COLS)),
                  _const_spec((1, ML_COLS)), _const_spec((L, L)), _const_spec((L, L))],
        out_specs=[row(ML_COLS), _const_spec((ML_HEADS, hd, 2 * hd)), _const_spec((ML_HEADS, 1, 1))],
        out_shape=[jax.ShapeDtypeStruct((t, ML_COLS), BF16),
                   jax.ShapeDtypeStruct((ML_HEADS, hd, 2 * hd), F32),
                   jax.ShapeDtypeStruct((ML_HEADS, 1, 1), F32)],
        scratch_shapes=[pltpu.VMEM((L, 2 * ML_COLS), F32), pltpu.VMEM((ML_HEADS, hd, 2 * hd), F32),
                        pltpu.VMEM((ML_HEADS, 1, 1), F32)],
        compiler_params=_params(("arbitrary",)),
        name="mlstm_prompt",
    )(mqk, mv, og, gates, gates_t, conv_w, conv_b.reshape(1, -1), ml_g.reshape(1, -1), tri, tri.T)


def _pattn_kernel(ii_ref, jj_ref, qt_ref, k_ref, vt_ref, bias_ref, dag_ref, lq1, lk1, lq2, lk2,
                  o_ref, m_sc, acc_sc, *, lam_init):
    s_id = pl.program_id(1)
    i = ii_ref[s_id]
    j = jj_ref[s_id]
    dq = DA_QK_DIM
    dv = DA_V_DIM
    den_row = (dv, 0)

    @pl.when(j == 0)
    def _():
        m_sc[...] = jnp.full_like(m_sc, NEG)
        acc_sc[...] = jnp.zeros_like(acc_sc)

    def step(use_bias):
        qt = qt_ref[...]
        k = k_ref[...]
        vt = vt_ref[...].astype(F32)
        sub_v = lax.broadcasted_iota(I32, vt.shape, 0)
        ones_row = ((sub_v == den_row[0]).astype(F32), (sub_v == den_row[1]).astype(F32))
        vt_aug = (jnp.where(sub_v < dv, vt, ones_row[0]).astype(BF16),
                  jnp.where(sub_v >= dv, vt, ones_row[1]).astype(BF16))
        tk = k.shape[0]
        sub = min(ATT_SUB, tk)
        work = [(idx, k0) for k0 in range(0, tk, sub) for idx in range(4)]

        def scores(idx, k0):
            c0 = idx * dq
            return jnp.dot(k[k0:k0 + sub, c0:c0 + dq], qt[c0:c0 + dq, :], preferred_element_type=F32)

        st_next = scores(*work[0])
        for n, (idx, k0) in enumerate(work):
            st = st_next
            if n + 1 < len(work):
                st_next = scores(*work[n + 1])
            hp = idx // 2
            if use_bias:
                st = st + bias_ref[hp, k0:k0 + sub, :]
            m_prev = m_sc[idx]
            m_new = jnp.maximum(m_prev, jnp.max(st, axis=0, keepdims=True))
            p = jnp.exp2(st - m_new)
            alpha = jnp.exp2(m_prev - m_new)
            m_sc[idx] = m_new
            acc_sc[idx] = acc_sc[idx] * alpha + jnp.dot(vt_aug[hp][:, k0:k0 + sub], p.astype(BF16),
                                                        preferred_element_type=F32)

    near = j >= i - 1
    pl.when(near)(lambda: step(True))
    pl.when(jnp.logical_not(near))(lambda: step(False))

    @pl.when(j == i)
    def _():
        lam = _lam(lq1, lk1, lq2, lk2, lam_init)
        for hp in range(2):
            a1 = acc_sc[2 * hp]
            a2 = acc_sc[2 * hp + 1]
            dr = den_row[hp]
            o = a1 / a1[dr:dr + 1, :] - lam * (a2 / a2[dr:dr + 1, :])
            o = o[hp * dv:(hp + 1) * dv, :]
            o = o * lax.rsqrt(jnp.mean(o * o, axis=0, keepdims=True) + EPS)
            o_ref[hp * dv:(hp + 1) * dv, :] = o * dag_ref[hp * dv:(hp + 1) * dv, :] * (1.0 - lam_init)


def _prompt_bias_tiles(rel_bias, tile):
    rb = rel_bias.astype(F32)
    rb = (rb - rb[NUM_BUCKETS - 1][None, :]) * LOG2E
    kpos = jnp.arange(tile)[:, None]
    qpos = jnp.arange(tile)[None, :]
    dist = jnp.stack([tile + qpos - kpos, qpos - kpos], axis=0)
    bucket = _t5_bucket(dist)[None]
    tiles = jnp.zeros((DA_HEADS, 2, tile, tile), F32)
    for b in range(NUM_BUCKETS - 1):
        tiles = jnp.where(bucket == b, rb[b][:, None, None, None], tiles)
    tiles = jnp.where((dist >= 0)[None], tiles, NEG)
    return tiles.reshape(DA_HEADS // 2, 2, 2, tile, tile)


def _prompt_attention(qt_bf, k_bf, vt_bf, rel_bias, da_g, lams, lam_init):
    t = k_bf.shape[0]
    tile = min(ATT_TILE, t)
    assert tile > MAX_DISTANCE or tile == t
    nq = t // tile
    ii, jj = [], []
    for i in range(nq):
        for j in range(i + 1):
            ii.append(i)
            jj.append(j)
    ii = jnp.asarray(np.array(ii, np.int32))
    jj = jnp.asarray(np.array(jj, np.int32))
    bias = _prompt_bias_tiles(rel_bias, tile)
    w = 2 * DA_V_DIM
    lam_specs = [_const_spec((1, DA_QK_DIM))] * 4
    grid_spec = pltpu.PrefetchScalarGridSpec(
        num_scalar_prefetch=2,
        grid=(DA_HEADS // 2, ii.shape[0]),
        in_specs=[
            pl.BlockSpec((w, tile), lambda p, s, ii, jj: (p, ii[s])),
            pl.BlockSpec((tile, w), lambda p, s, ii, jj: (jj[s], p)),
            pl.BlockSpec((w, tile), lambda p, s, ii, jj: (p, jj[s])),
            pl.BlockSpec((None, 2, None, tile, tile),
                         lambda p, s, ii, jj: (p, 0, jnp.clip(jj[s] - ii[s] + 1, 0, 1), 0, 0)),
            pl.BlockSpec((w, 1), lambda p, s, ii, jj: (p, 0)),
        ] + lam_specs,
        out_specs=pl.BlockSpec((w, tile), lambda p, s, ii, jj: (p, ii[s])),
        scratch_shapes=[pltpu.VMEM((4, 1, tile), F32), pltpu.VMEM((4, w, tile), F32)],
    )
    return pl.pallas_call(
        functools.partial(_pattn_kernel, lam_init=lam_init),
        grid_spec=grid_spec,
        out_shape=jax.ShapeDtypeStruct((DA_COLS, t), F32),
        compiler_params=_params(("parallel", "arbitrary")),
        name="prompt_attention",
    )(ii, jj, qt_bf, k_bf, vt_bf, bias, da_g.reshape(-1, 1), *lams)


def _sinproj_kernel(x_ref, sh_ref, sc_ref, g_ref, wqkv_ref, wml_ref, wg_ref, bgt_ref,
                    qkvt_ref, mqk_ref, mvot_ref, gatest_ref):
    h = _modulated_norm(x_ref[...], g_ref[...], sh_ref[...], sc_ref[...])
    hb = h.astype(BF16)
    qkvt_ref[...] = lax.dot_general(wqkv_ref[...], hb, NT, preferred_element_type=F32)
    mqk_ref[...] = lax.dot_general(hb, wml_ref[0:2 * ML_COLS, :], NT, preferred_element_type=F32)
    mvot_ref[...] = lax.dot_general(wml_ref[2 * ML_COLS:4 * ML_COLS, :], hb, NT, preferred_element_type=F32)
    grawt = lax.dot_general(wg_ref[...], h, NT, precision=HI, preferred_element_type=F32) + bgt_ref[...]
    sub = lax.broadcasted_iota(I32, grawt.shape, 0)
    gatest_ref[...] = jnp.where(sub < ML_HEADS, grawt, _log_sigmoid(grawt))


def _sample_inproj(x, shift, scale, g, wqkv_t, wml_t, wg_t, b_gate):
    b, d = x.shape
    return pl.pallas_call(
        _sinproj_kernel,
        out_shape=[jax.ShapeDtypeStruct((3 * DA_COLS, b), F32), jax.ShapeDtypeStruct((b, 2 * ML_COLS), F32),
                   jax.ShapeDtypeStruct((2 * ML_COLS, b), F32), jax.ShapeDtypeStruct((N_GATES, b), F32)],
        compiler_params=pltpu.CompilerParams(vmem_limit_bytes=VMEM_LIMIT),
        name="sample_inproj",
    )(x, shift, scale, g, wqkv_t, wml_t, wg_t, b_gate.reshape(N_GATES, 1))


def _sconv_kernel(mqk_ref, buf_ref, cw_ref, cb_ref, qkt_ref, new_ref):
    x = mqk_ref[...]
    y = cb_ref[...] + buf_ref[0] * cw_ref[0:1, :]
    y = y + buf_ref[1] * cw_ref[1:2, :]
    y = y + buf_ref[2] * cw_ref[2:3, :]
    y = y + x * cw_ref[3:4, :]
    a = y * _sigmoid(y)
    lane = lax.broadcasted_iota(I32, a.shape, 1)
    qkt_ref[...] = jnp.where(lane < ML_COLS, a, a * K_SCALE).T
    new_ref[0] = buf_ref[1]
    new_ref[1] = buf_ref[2]
    new_ref[2] = x


def _sample_conv(mqk, conv_buf, conv_w, conv_b):
    b, n = mqk.shape
    return pl.pallas_call(
        _sconv_kernel,
        out_shape=[jax.ShapeDtypeStruct((n, b), F32), jax.ShapeDtypeStruct(conv_buf.shape, F32)],
        compiler_params=pltpu.CompilerParams(vmem_limit_bytes=VMEM_LIMIT),
        name="sample_conv",
    )(mqk, conv_buf, conv_w, conv_b.reshape(1, -1))


def _smlstm_kernel(q_ref, k_ref, v_ref, og_ref, ig_ref, lf_ref, m_ref, c_ref, n_ref, mlg_ref,
                   co_ref, no_ref, mo_ref, ho_ref):
    hd = ML_HEAD_DIM
    ig = ig_ref[...]
    lf = lf_ref[...]
    m_old = m_ref[...]
    m_t = jnp.maximum(m_old + lf, ig)
    a = jnp.exp(ig - m_t)
    w = jnp.exp(m_old + lf - m_t)
    v = v_ref[...]

    def body(d, num):
        cn = w * c_ref[d] + (a * k_ref[pl.ds(d, 1), :]) * v
        co_ref[d] = cn
        return num + q_ref[pl.ds(d, 1), :] * cn

    num = lax.fori_loop(0, hd, body, jnp.zeros(v.shape, F32))
    nn = w * n_ref[...] + a * k_ref[...]
    den = jnp.sum(q_ref[...] * nn, axis=0, keepdims=True)
    hv = num / jnp.maximum(jnp.abs(den), jnp.exp(-m_t))
    hn = hv * lax.rsqrt(jnp.mean(hv * hv, axis=0, keepdims=True) + EPS) * mlg_ref[...]
    no_ref[...] = nn
    mo_ref[...] = m_t
    ho_ref[...] = hn * _sigmoid(og_ref[...])


def _sample_mlstm(qk_t, v_t, og_t, gates_t, m_t, c_t, n_t, ml_g):
    b = qk_t.shape[1]
    h, d = ML_HEADS, ML_HEAD_DIM
    head_rows = lambda off: pl.BlockSpec((d, b), lambda i: (i + off, 0))
    vec = pl.BlockSpec((None, 1, b), lambda i: (i, 0, 0))
    return pl.pallas_call(
        _smlstm_kernel,
        grid=(h,),
        in_specs=[head_rows(0), head_rows(h), head_rows(0), head_rows(0),
                  vec, pl.BlockSpec((None, 1, b), lambda i: (i + h, 0, 0)), vec,
                  pl.BlockSpec((None, d, d, b), lambda i: (i, 0, 0, 0)),
                  pl.BlockSpec((None, d, b), lambda i: (i, 0, 0)),
                  pl.BlockSpec((None, d, 1), lambda i: (i, 0, 0))],
        out_specs=[pl.BlockSpec((None, d, d, b), lambda i: (i, 0, 0, 0)),
                   pl.BlockSpec((None, d, b), lambda i: (i, 0, 0)), vec, head_rows(0)],
        out_shape=[jax.ShapeDtypeStruct((h, d, d, b), F32), jax.ShapeDtypeStruct((h, d, b), F32),
                   jax.ShapeDtypeStruct((h, 1, b), F32), jax.ShapeDtypeStruct((h * d, b), F32)],
        compiler_params=_params(("parallel",)),
        name="sample_mlstm",
    )(qk_t, qk_t, v_t, og_t, gates_t.reshape(2 * h, 1, b), gates_t.reshape(2 * h, 1, b),
      m_t.reshape(h, 1, b), c_t, n_t, ml_g.reshape(h, d, 1))


def _sattn_kernel(pt_ref, qt_ref, knt_ref, vnt_ref, bpage_ref, bnew_ref, dag_ref, lq1, lk1, lq2, lk2,
                  ck_ref, cv_ref, o_ref, s_sc, a_sc, acc, qb, kbuf, vbuf, ksem, vsem, *, n_pg, lam_init):
    b = pl.program_id(0)
    ph = pl.program_id(1)
    g = pl.program_id(2)
    n_b = pl.num_programs(0)
    n_g = pl.num_programs(2)
    dq = DA_QK_DIM
    dv = DA_V_DIM
    nh = DA_HEADS
    n_pages = s_sc.shape[0] - 1
    lane = lax.broadcasted_iota(I32, qt_ref.shape, 1)

    def group_copies(cache_ref, buf, sem, bq, gq, slot):
        return [pltpu.make_async_copy(cache_ref.at[pt_ref[gq * n_pg + i, bq]], buf.at[slot, i], sem.at[slot])
                for i in range(n_pg)]

    def start_group(phq, bq, gq):
        slot = (bq * n_g + gq) % 2

        @pl.when(phq == 0)
        def _():
            for cp in group_copies(ck_ref, kbuf, ksem, bq, gq, slot):
                cp.start()

        @pl.when(phq == 1)
        def _():
            for cp in group_copies(cv_ref, vbuf, vsem, bq, gq, slot):
                cp.start()

    step = (b * 2 + ph) * n_g + g
    slot = (b * n_g + g) % 2

    @pl.when(step == 0)
    def _():
        start_group(ph, b, g)

    @pl.when(step + 1 < n_b * 2 * n_g)
    def _():
        nxt = step + 1
        start_group((nxt // n_g) % 2, nxt // (2 * n_g), nxt % n_g)

    k_refs = [kbuf.at[slot, i] for i in range(n_pg)]
    v_refs = [vbuf.at[slot, i] for i in range(n_pg)]

    def column(ref):
        return jnp.sum(jnp.where(lane == b, ref[...], 0.0), axis=1, keepdims=True)

    @pl.when((b == 0) & (ph == 0) & (g == 0))
    def _():
        o_ref[...] = jnp.zeros_like(o_ref)

    @pl.when((ph == 0) & (g == 0))
    def _():
        qb[...] = jnp.broadcast_to(column(qt_ref) * Q_SCALE, qb.shape)

    @pl.when(ph == 0)
    def _():
        for cp in group_copies(ck_ref, kbuf, ksem, b, g, slot):
            cp.wait()
        for c in range(2 * nh):
            qc = qb[c * dq:(c + 1) * dq, :]
            srow = (c % 2) * nh + c // 2
            for i in range(n_pg):
                prod = k_refs[i][c * dq:(c + 1) * dq, :] * qc
                s_sc[g * n_pg + i, srow:srow + 1, :] = jnp.sum(prod, axis=0, keepdims=True)

        @pl.when(g == n_g - 1)
        def _():
            s_sc[n_pages - 1] = s_sc[n_pages - 1] + bpage_ref[...]
            prodn = column(knt_ref) * qb[:, 0:1]
            sub = lax.broadcasted_iota(I32, (2 * nh, 1), 0)
            ln = bnew_ref[...]
            for r in range(2 * nh):
                c = (r % nh) * 2 + r // nh
                ln = ln + jnp.where(sub == r, jnp.sum(prodn[c * dq:(c + 1) * dq, :], axis=0, keepdims=True), 0.0)
            lane_s = lax.broadcasted_iota(I32, (2 * nh, LANES), 1)
            s_sc[n_pages] = jnp.where(lane_s == 0, ln, NEG)
            s_all = s_sc[...]
            mx = jnp.max(jnp.max(s_all, axis=0), axis=1, keepdims=True)
            p = jnp.exp(s_all - mx)
            l = jnp.sum(jnp.sum(p, axis=0), axis=1, keepdims=True)
            lam = _lam(lq1, lk1, lq2, lk2, lam_init)
            a_sc[...] = p[:, 0:nh, :] / l[0:nh, :] - lam * (p[:, nh:2 * nh, :] / l[nh:2 * nh, :])
            acc[...] = jnp.zeros_like(acc)

    @pl.when(ph == 1)
    def _():
        for cp in group_copies(cv_ref, vbuf, vsem, b, g, slot):
            cp.wait()
        for h in range(nh):
            part = acc[h * dv:(h + 1) * dv, :]
            for i in range(n_pg):
                part = part + v_refs[i][h * dv:(h + 1) * dv, :] * a_sc[g * n_pg + i, h:h + 1, :]
            acc[h * dv:(h + 1) * dv, :] = part

        @pl.when(g == n_g - 1)
        def _():
            out = jnp.sum(acc[...], axis=1, keepdims=True)
            vn = column(vnt_ref)
            a_new = a_sc[n_pages][:, 0:1]
            pieces = []
            for h in range(nh):
                oh = out[h * dv:(h + 1) * dv, :] + a_new[h:h + 1, :] * vn[h * dv:(h + 1) * dv, :]
                pieces.append(oh * lax.rsqrt(jnp.mean(oh * oh, axis=0, keepdims=True) + EPS))
            res = jnp.concatenate(pieces, axis=0) * dag_ref[...] * (1.0 - lam_init)
            o_ref[...] = jnp.where(lane == b, res, o_ref[...])


def _sample_attention(q_t, kn_t, vn_t, ck_t, cv_t, page_table_t, rel_bias, da_g, lams, lam_init):
    w, b = q_t.shape
    n_pages = page_table_t.shape[0]
    n_pg = min(PAGES_PER_STEP, n_pages)
    n_g = n_pages // n_pg
    ps = PAGE_SIZE
    past = n_pages * ps
    nh = DA_HEADS
    tab = _bias_by_distance(rel_bias, ps + 1)
    bpage = jnp.tile(tab[ps - jnp.arange(ps)].T, (2, 1))
    bnew = jnp.tile(tab[0:1].T, (2, 1))

    in_specs = [_const_spec((w, b))] * 3 + [_const_spec((2 * nh, ps)), _const_spec((2 * nh, 1)),
                                            _const_spec((w, 1))] + [_const_spec((1, DA_QK_DIM))] * 4
    in_specs += [pl.BlockSpec(memory_space=pl.ANY)] * 2
    grid_spec = pltpu.PrefetchScalarGridSpec(
        num_scalar_prefetch=1,
        grid=(b, 2, n_g),
        in_specs=in_specs,
        out_specs=_const_spec((w, b)),
        scratch_shapes=[pltpu.VMEM((n_pages + 1, 2 * nh, ps), F32), pltpu.VMEM((n_pages + 1, nh, ps), F32),
                        pltpu.VMEM((w, ps), F32), pltpu.VMEM((w, ps), F32),
                        pltpu.VMEM((2, n_pg, w, ps), F32), pltpu.VMEM((2, n_pg, w, ps), F32),
                        pltpu.SemaphoreType.DMA((2,)), pltpu.SemaphoreType.DMA((2,))],
    )
    return pl.pallas_call(
        functools.partial(_sattn_kernel, n_pg=n_pg, lam_init=lam_init),
        grid_spec=grid_spec,
        out_shape=jax.ShapeDtypeStruct((w, b), F32),
        compiler_params=_params(("arbitrary", "arbitrary", "arbitrary")),
        name="sample_attention",
    )(page_table_t, q_t, kn_t, vn_t, bpage, bnew, da_g.reshape(w, 1), *lams, ck_t, cv_t)


def _outproj_kernel(ap_ref, mop_ref, xp_ref, g1p_ref, sh2p_ref, sc2p_ref, g2p_ref,
                    as_ref, mos_ref, xs_ref, g1s_ref, sh2s_ref, sc2s_ref, g2s_ref,
                    wo_ref, n2_ref, wrt_ref, wsg_ref, wsu_ref, wsd_ref,
                    x1p_ref, x1s_ref, h2t_ref, lt_ref):
    i = pl.program_id(0)
    n_prompt = pl.num_programs(0) - 1

    def block(a, mo, x, g1, sh2, sc2, g2, out_ref):
        half = a.shape[1]
        mix = jnp.dot(a, wo_ref[0:half, :], preferred_element_type=F32)
        mix = mix + jnp.dot(mo, wo_ref[half:2 * half, :], preferred_element_type=F32)
        x1 = x + g1 * mix
        h2 = _modulated_norm(x1, n2_ref[...], sh2, sc2)
        lt_ref[...] = lax.dot_general(wrt_ref[...], h2, NT, precision=HI, preferred_element_type=F32)
        tm = h2.shape[0]
        for s in range(h2.shape[1] // LANES):
            h2t_ref[pl.ds(s, tm, stride=SUBLANES), :] = h2[:, s * LANES:(s + 1) * LANES]
        hb = h2.astype(BF16)
        gate = jnp.dot(hb, wsg_ref[...], preferred_element_type=F32)
        up = jnp.dot(hb, wsu_ref[...], preferred_element_type=F32)
        act = (gate * _sigmoid(gate) * up).astype(BF16)
        shared = jnp.dot(act, wsd_ref[...], preferred_element_type=F32)
        out_ref[...] = x1 + g2 * shared

    @pl.when(i < n_prompt)
    def _():
        block(ap_ref[...].T.astype(BF16), mop_ref[...], xp_ref[...], g1p_ref[...], sh2p_ref[...], sc2p_ref[...],
              g2p_ref[...], x1p_ref)

    @pl.when(i == n_prompt)
    def _():
        block(as_ref[...].T.astype(BF16), mos_ref[...].T.astype(BF16), xs_ref[...], g1s_ref[...], sh2s_ref[...],
              sc2s_ref[...], g2s_ref[...], x1s_ref)


def _outproj(a_p, mo_p, x_p, mod_p, a_ts, mo_ts, x_s, mod_s, w_out, n2, w_router_t, wsg, wsu, wsd):
    t, d = x_p.shape
    bs = x_s.shape[0]
    tm = TOK_BLK
    assert bs == tm and t % tm == 0
    nbp = t // tm
    e = w_router_t.shape[0]
    nsl = d // LANES
    prow = lambda n: pl.BlockSpec((tm, n), lambda i: (jnp.minimum(i, nbp - 1), 0))
    pcol = pl.BlockSpec((DA_COLS, tm), lambda i: (0, jnp.minimum(i, nbp - 1)))
    in_specs = ([pcol, prow(ML_COLS), prow(d)] + [_const_spec((1, d))] * 4
                + [_const_spec((DA_COLS, bs)), _const_spec((ML_COLS, bs)), _const_spec((bs, d))]
                + [_const_spec((bs, d))] * 4
                + [_const_spec(w_out.shape), _const_spec((1, d)), _const_spec(w_router_t.shape),
                   _const_spec(wsg.shape), _const_spec(wsu.shape), _const_spec(wsd.shape)])
    return pl.pallas_call(
        _outproj_kernel,
        grid=(nbp + 1,),
        in_specs=in_specs,
        out_specs=[prow(d), _const_spec((bs, d)), pl.BlockSpec((tm * nsl, LANES), lambda i: (i, 0)),
                   pl.BlockSpec((e, tm), lambda i: (0, i))],
        out_shape=[jax.ShapeDtypeStruct((t, d), F32), jax.ShapeDtypeStruct((bs, d), F32),
                   jax.ShapeDtypeStruct(((t + bs) * nsl, LANES), F32), jax.ShapeDtypeStruct((e, t + bs), F32)],
        compiler_params=_params(("arbitrary",)),
        name="outproj",
    )(a_p, mo_p, x_p, *mod_p, a_ts, mo_ts, x_s, *mod_s, w_out, n2, w_router_t, wsg, wsu, wsd)


def _first_max(cur, iota, n, axis=0):
    t = jnp.max(cur, axis=axis, keepdims=True)
    idx = jnp.min(jnp.where(cur == t, iota, float(n)), axis=axis, keepdims=True)
    return t, idx


def _route_kernel(lt_ref, bias_ref, ut_ref, ones_ref, eidx_ref, w_ref, rank_ref, cnt_ref, carry):
    i = pl.program_id(0)

    @pl.when(i == 0)
    def _():
        carry[...] = jnp.zeros_like(carry)

    aff = _sigmoid(lt_ref[...])
    sel = aff + bias_ref[...]
    n_e, n_t = aff.shape
    gsz = n_e // N_GROUPS
    e_iota = lax.broadcasted_iota(I32, (n_e, n_t), 0).astype(F32)
    g_iota = lax.broadcasted_iota(I32, (N_GROUPS, n_t), 0).astype(F32)
    in_iota = lax.broadcasted_iota(I32, (gsz, n_t), 0).astype(F32)
    ninf = -jnp.inf
    gscore = jnp.zeros((N_GROUPS, n_t), F32)
    for g in range(N_GROUPS):
        blk = sel[g * gsz:(g + 1) * gsz, :]
        t1, i1 = _first_max(blk, in_iota, gsz)
        t2 = jnp.max(jnp.where(in_iota == i1, ninf, blk), axis=0, keepdims=True)
        gscore = jnp.where(g_iota == float(g), t1 + t2, gscore)
    gmask = jnp.zeros((N_GROUPS, n_t), F32)
    cur = gscore
    for _ in range(TOPK_GROUPS):
        _, gi = _first_max(cur, g_iota, N_GROUPS)
        hit = g_iota == gi
        gmask = jnp.where(hit, 1.0, gmask)
        cur = jnp.where(hit, ninf, cur)
    cur = jnp.concatenate(
        [jnp.where(gmask[g:g + 1, :] > 0.0, sel[g * gsz:(g + 1) * gsz, :], ninf) for g in range(N_GROUPS)], axis=0)
    hits, ws = [], []
    msum = jnp.zeros((n_e, n_t), F32)
    for k in range(TOP_K):
        _, ei = _first_max(cur, e_iota, n_e)
        hit = e_iota == ei
        ws.append(jnp.sum(jnp.where(hit, aff, 0.0), axis=0, keepdims=True))
        cur = jnp.where(hit, ninf, cur)
        msum = msum + hit.astype(F32)
        hits.append(hit)
        eidx_ref[k:k + 1, :] = ei.astype(I32)
    wsum = ws[0]
    for k in range(1, TOP_K):
        wsum = wsum + ws[k]
    mb = msum.astype(BF16)
    base = carry[...] + jnp.dot(mb, ut_ref[...], preferred_element_type=F32)
    for k in range(TOP_K):
        w_ref[k:k + 1, :] = ws[k] / wsum * ROUTED_SCALE
        rank_ref[k:k + 1, :] = jnp.sum(jnp.where(hits[k], base, 0.0), axis=0, keepdims=True).astype(I32)
    carry[...] = carry[...] + jnp.dot(mb, ones_ref[...], preferred_element_type=F32)
    cnt_ref[...] = carry[...]


def _route(logits_t, router_bias):
    n_e, t_all = logits_t.shape
    tb = TOK_BLK
    ut = jnp.asarray(np.triu(np.ones((tb, tb), np.float32), 1), dtype=BF16)
    ones = jnp.ones((tb, tb), BF16)
    bias_b = jnp.broadcast_to(router_bias.astype(F32)[:, None], (n_e, tb))
    tok = pl.BlockSpec((TOP_K, tb), lambda i: (0, i))
    return pl.pallas_call(
        _route_kernel,
        grid=(t_all // tb,),
        in_specs=[pl.BlockSpec((n_e, tb), lambda i: (0, i)), _const_spec((n_e, tb)), _const_spec((tb, tb)),
                  _const_spec((tb, tb))],
        out_specs=[tok, tok, tok, _const_spec((n_e, tb))],
        out_shape=[jax.ShapeDtypeStruct((TOP_K, t_all), I32), jax.ShapeDtypeStruct((TOP_K, t_all), F32),
                   jax.ShapeDtypeStruct((TOP_K, t_all), I32), jax.ShapeDtypeStruct((n_e, tb), F32)],
        scratch_shapes=[pltpu.VMEM((n_e, tb), F32)],
        compiler_params=_params(("arbitrary",)),
        name="moe_route",
    )(logits_t, bias_b, ut, ones)


def _pos_kernel(eidx_ref, rank_ref, offs_ref, pos_ref):
    offs = offs_ref[...]
    e_iota = lax.broadcasted_iota(I32, offs.shape, 0)
    for k in range(TOP_K):
        hit = e_iota == eidx_ref[k:k + 1, :]
        base = jnp.sum(jnp.where(hit, offs, 0.0), axis=0, keepdims=True).astype(I32)
        pos_ref[k:k + 1, :] = base + rank_ref[k:k + 1, :]


def _positions(eidx, rank, offs):
    t_all = eidx.shape[1]
    tb = TOK_BLK
    n_e = offs.shape[0]
    tok = pl.BlockSpec((TOP_K, tb), lambda i: (0, i))
    return pl.pallas_call(
        _pos_kernel,
        grid=(t_all // tb,),
        in_specs=[tok, tok, _const_spec((n_e, tb))],
        out_specs=tok,
        out_shape=jax.ShapeDtypeStruct((TOP_K, t_all), I32),
        compiler_params=_params(("parallel",)),
        name="moe_positions",
    )(eidx, rank, jnp.broadcast_to(offs.astype(F32)[:, None], (n_e, tb)))


def _tile_start(row):
    start = row * SUBLANES
    return start if isinstance(start, int) else pl.multiple_of(start, SUBLANES)


def _row_copy(src, src_row, dst, dst_row, sem):
    return pltpu.make_async_copy(src.at[pl.ds(_tile_start(src_row), SUBLANES)],
                                 dst.at[pl.ds(_tile_start(dst_row), SUBLANES)], sem)


def _dispatch_kernel(pos_ref, h2t_ref, xs_in_ref, xs_ref, sem):
    del xs_in_ref
    tb = pos_ref.shape[1]

    def issue(t, carry):
        for k in range(TOP_K):
            _row_copy(h2t_ref, t, xs_ref, pos_ref[k, t], sem).start(priority=k % 2)
        return carry

    lax.fori_loop(0, tb, issue, 0)

    def drain(t, carry):
        for k in range(TOP_K):
            _row_copy(h2t_ref, 0, xs_ref, 0, sem).wait()
        return carry

    lax.fori_loop(0, tb, drain, 0)


def _dispatch(pos, h2t, n_rows):
    t_all = pos.shape[1]
    tb = TOK_BLK
    xs0 = jnp.zeros((n_rows * SUBLANES, LANES), F32)
    return pl.pallas_call(
        _dispatch_kernel,
        grid=(t_all // tb,),
        in_specs=[pl.BlockSpec((TOP_K, tb), lambda i: (0, i), memory_space=pltpu.SMEM),
                  pl.BlockSpec((tb * SUBLANES, LANES), lambda i: (i, 0)), pl.BlockSpec(memory_space=pl.ANY)],
        out_specs=pl.BlockSpec(memory_space=pl.ANY),
        out_shape=jax.ShapeDtypeStruct(xs0.shape, F32),
        input_output_aliases={2: 0},
        scratch_shapes=[pltpu.SemaphoreType.DMA(())],
        compiler_params=_params(("arbitrary",)),
        name="moe_dispatch",
    )(pos, h2t, xs0)


def _experts_kernel(be_ref, nv_ref, xs_ref, wg_ref, wu_ref, wd_ref, ys_ref, wgb, wub, wdb):
    i = pl.program_id(0)

    @pl.when(i < nv_ref[0])
    def _():
        @pl.when((i == 0) | (be_ref[i] != be_ref[jnp.maximum(i - 1, 0)]))
        def _():
            wgb[...] = wg_ref[...].astype(BF16)
            wub[...] = wu_ref[...].astype(BF16)
            wdb[...] = wd_ref[...].astype(BF16)

        bm = xs_ref.shape[0] // SUBLANES
        nsl = wg_ref.shape[0] // LANES
        x = jnp.concatenate([xs_ref[pl.ds(s, bm, stride=SUBLANES), :] for s in range(nsl)], axis=1).astype(BF16)
        gate = jnp.dot(x, wgb[...], preferred_element_type=F32)
        up = jnp.dot(x, wub[...], preferred_element_type=F32)
        act = (gate * _sigmoid(gate) * up).astype(BF16)
        y = jnp.dot(act, wdb[...], preferred_element_type=F32)
        for s in range(nsl):
            ys_ref[pl.ds(s, bm, stride=SUBLANES), :] = y[:, s * LANES:(s + 1) * LANES]


def _experts(block_e, n_valid, xs, w_g, w_u, w_d, n_blocks):
    bm = MOE_BM
    d, de = w_g.shape[1], w_g.shape[2]
    rows = pl.BlockSpec((bm * SUBLANES, LANES), lambda i, be, nv: (jnp.minimum(i, nv[0] - 1), 0))
    grid_spec = pltpu.PrefetchScalarGridSpec(
        num_scalar_prefetch=2,
        grid=(n_blocks,),
        in_specs=[rows,
                  pl.BlockSpec((None, d, de), lambda i, be, nv: (be[i], 0, 0)),
                  pl.BlockSpec((None, d, de), lambda i, be, nv: (be[i], 0, 0)),
                  pl.BlockSpec((None, de, d), lambda i, be, nv: (be[i], 0, 0))],
        out_specs=rows,
        scratch_shapes=[pltpu.VMEM((d, de), BF16), pltpu.VMEM((d, de), BF16), pltpu.VMEM((de, d), BF16)],
    )
    return pl.pallas_call(
        _experts_kernel,
        grid_spec=grid_spec,
        out_shape=jax.ShapeDtypeStruct(xs.shape, F32),
        input_output_aliases={2: 0},
        compiler_params=_params(("arbitrary",)),
        name="moe_experts",
    )(block_e, n_valid, xs, w_g, w_u, w_d)


def _combine_kernel(pos_ref, w_ref, x1p_ref, g2p_ref, x1s_ref, g2s_ref, fg_ref, ys_ref, yp_ref, ysm_ref, buf, sem):
    i = pl.program_id(0)
    n_prompt = pl.num_programs(0) - 1
    tb = pos_ref.shape[1]
    nsl = x1p_ref.shape[1] // LANES

    def issue(t, carry):
        for k in range(TOP_K):
            pltpu.make_async_copy(ys_ref.at[pl.ds(_tile_start(pos_ref[k, t]), SUBLANES)],
                                  buf.at[k, pl.ds(_tile_start(t), SUBLANES)], sem).start(priority=k % 2)
        return carry

    lax.fori_loop(0, tb, issue, 0)

    def drain(t, carry):
        for k in range(TOP_K):
            pltpu.make_async_copy(ys_ref.at[pl.ds(0, SUBLANES)], buf.at[k, pl.ds(0, SUBLANES)], sem).wait()
        return carry

    lax.fori_loop(0, tb, drain, 0)

    w = w_ref[...]
    routed = None
    for k in range(TOP_K):
        yk = jnp.concatenate([buf[k, pl.ds(s, tb, stride=SUBLANES), :] for s in range(nsl)], axis=1)
        term = yk * w[:, k:k + 1]
        routed = term if routed is None else routed + term

    def finish(x1, g2, out_ref):
        x2 = x1 + g2 * routed
        out_ref[...] = x2 * lax.rsqrt(jnp.mean(x2 * x2, axis=-1, keepdims=True) + EPS) * fg_ref[...]

    @pl.when(i < n_prompt)
    def _():
        finish(x1p_ref[...], g2p_ref[...], yp_ref)

    @pl.when(i == n_prompt)
    def _():
        finish(x1s_ref[...], g2s_ref[...], ysm_ref)


def _combine(pos, w_tm, x1_p, g2_p, x1_s, g2_s, final_g, ys):
    t, d = x1_p.shape
    bs = x1_s.shape[0]
    tb = TOK_BLK
    assert bs == tb and t % tb == 0
    nbp = t // tb
    prow = pl.BlockSpec((tb, d), lambda i: (jnp.minimum(i, nbp - 1), 0))
    return pl.pallas_call(
        _combine_kernel,
        grid=(nbp + 1,),
        in_specs=[pl.BlockSpec((TOP_K, tb), lambda i: (0, i), memory_space=pltpu.SMEM),
                  pl.BlockSpec((tb, TOP_K), lambda i: (i, 0)),
                  prow, _const_spec((1, d)), _const_spec((bs, d)), _const_spec((bs, d)), _const_spec((1, d)),
                  pl.BlockSpec(memory_space=pl.ANY)],
        out_specs=[prow, _const_spec((bs, d))],
        out_shape=[jax.ShapeDtypeStruct((t, d), F32), jax.ShapeDtypeStruct((bs, d), F32)],
        scratch_shapes=[pltpu.VMEM((TOP_K, tb * SUBLANES, LANES), F32), pltpu.SemaphoreType.DMA(())],
        compiler_params=_params(("arbitrary",)),
        name="moe_combine",
    )(pos, w_tm, x1_p, g2_p, x1_s, g2_s, final_g.reshape(1, d), ys)


def kernel(x_prompt, x_sample, cache_k, cache_v, state_C, state_n, state_m, state_conv, page_table, c_prompt, c_sample, w_ada, b_ada, norm1_g, norm2_g, w_in, conv_w, conv_b, lambda_q1, lambda_k1, lambda_q2, lambda_k2, da_norm_g, rel_bias, ml_b_i, ml_b_f, ml_norm_g, w_out, w_router, router_bias, w_exp_gate, w_exp_up, w_exp_down, w_sh_gate, w_sh_up, w_sh_down, final_norm_g):
    bp, t, d = x_prompt.shape
    bs, ts, _ = x_sample.shape
    depth = w_ada.shape[0]
    assert bp == 1 and ts == 1 and depth == 1
    assert d % LANES == 0 and (t + bs) % TOK_BLK == 0 and t % TOK_BLK == 0
    l = 0
    lam_init = 0.8 - 0.6 * math.exp(-0.3 * l)
    lams = [a[l].reshape(1, -1).astype(F32) for a in (lambda_q1, lambda_k1, lambda_q2, lambda_k2)]
    t_all = t + bs
    hd = ML_HEAD_DIM

    pad = (-(bp + bs)) % SUBLANES
    c_all = jnp.concatenate([c_prompt, c_sample, jnp.zeros((pad, d), F32)], axis=0)
    mod = _adaln(c_all, w_ada[l], b_ada[l])
    modp = [mod[0:1, i * d:(i + 1) * d] for i in range(6)]
    mods = [mod[1:1 + bs, i * d:(i + 1) * d] for i in range(6)]

    w_in_t = w_in[l].T
    c = DA_COLS
    wqkv_t = w_in_t[0:3 * c].astype(BF16)
    wml_t = w_in_t[3 * c:3 * c + 4 * ML_COLS].astype(BF16)
    wg_t = w_in_t[3 * c + 4 * ML_COLS:]
    b_gate = jnp.concatenate([ml_b_i[l], ml_b_f[l]]).astype(F32)
    w_out_b = w_out[l].astype(BF16)
    w_router_t = w_router[l].T
    wsg, wsu, wsd = w_sh_gate[l].astype(BF16), w_sh_up[l].astype(BF16), w_sh_down[l].astype(BF16)
    n1 = norm1_g[l].reshape(1, d)
    n2 = norm2_g[l].reshape(1, d)

    xp = x_prompt.reshape(t, d)
    (qtb, krb, ktf, vtf, vtb, mqk, mv, mo_raw, gates, gates_t) = _inproj(
        xp, modp[0], modp[1], n1, wqkv_t, wml_t, wg_t, b_gate)
    attn_p = _prompt_attention(qtb, krb, vtb, rel_bias, da_norm_g[l], lams, lam_init)
    mo_p, s_p, m_p = _mlstm_prompt(mqk, mv, mo_raw, gates, gates_t, conv_w[l], conv_b[l], ml_norm_g[l])

    xs = x_sample.reshape(bs, d)
    qkvt_s, mqk_s, mvot_s, gates_ts = _sample_inproj(xs, mods[0], mods[1], n1, wqkv_t, wml_t, wg_t, b_gate)
    conv_buf = jnp.transpose(state_conv[l], (1, 0, 2))
    qkt_s, conv_new_s = _sample_conv(mqk_s, conv_buf, conv_w[l], conv_b[l])
    c_t = jnp.transpose(state_C[l], (1, 2, 3, 0))
    n_t = jnp.transpose(state_n[l], (1, 2, 0))
    m_t = state_m[l].T
    c_s, n_s, m_s, mo_ts = _sample_mlstm(qkt_s, mvot_s[0:ML_COLS], mvot_s[ML_COLS:2 * ML_COLS], gates_ts, m_t, c_t,
                                         n_t, ml_norm_g[l])
    n_pool = cache_k.shape[1]
    ck_t = jnp.transpose(cache_k[l], (0, 2, 3, 1)).reshape(n_pool, c, PAGE_SIZE)
    cv_t = jnp.transpose(cache_v[l], (0, 2, 3, 1)).reshape(n_pool, c, PAGE_SIZE)
    attn_ts = _sample_attention(qkvt_s[0:c], qkvt_s[c:2 * c], qkvt_s[2 * c:3 * c], ck_t, cv_t,
                                page_table.T, rel_bias, da_norm_g[l], lams, lam_init)

    xs1_p, xs1_s, h2t, logits_t = _outproj(attn_p, mo_p, xp, modp[2:6], attn_ts, mo_ts, xs, mods[2:6], w_out_b, n2,
                                           w_router_t, wsg, wsu, wsd)

    n_e = w_router.shape[2]
    eidx, wts, rank, cnt = _route(logits_t, router_bias[l])
    counts = cnt[:, 0].astype(I32)
    padded = (counts + MOE_BM - 1) // MOE_BM * MOE_BM
    pad_end = jnp.cumsum(padded)
    offs = pad_end - padded
    n_blocks = -(-(t_all * TOP_K) // MOE_BM) + n_e
    block_e = jnp.minimum(jnp.searchsorted(pad_end, jnp.arange(n_blocks) * MOE_BM, side='right'),
                          n_e - 1).astype(I32)
    n_valid = (pad_end[-1:] // MOE_BM).astype(I32)
    pos = _positions(eidx, rank, offs)
    xs_sorted = _dispatch(pos, h2t, n_blocks * MOE_BM)
    ys = _experts(block_e, n_valid, xs_sorted, w_exp_gate[l], w_exp_up[l], w_exp_down[l], n_blocks)
    w_tm = wts.T
    y_p, y_s = _combine(pos, w_tm, xs1_p, modp[5], xs1_s, mods[5], final_norm_g, ys)

    def per_head(a_t, n_feat):
        return jnp.transpose(a_t.reshape(-1, n_feat, a_t.shape[1]), (2, 0, 1))

    return (
        y_p.reshape(bp, t, d),
        y_s.reshape(bs, ts, d),
        per_head(ktf, 2 * DA_QK_DIM).reshape(1, bp, t, DA_HEADS, 2 * DA_QK_DIM),
        per_head(vtf, DA_V_DIM).reshape(1, bp, t, DA_HEADS, DA_V_DIM),
        s_p[:, :, :hd].reshape(1, bp, ML_HEADS, hd, hd),
        s_p[:, :, hd].reshape(1, bp, ML_HEADS, hd),
        m_p.reshape(1, bp, ML_HEADS),
        mqk[t - (CONV_W - 1):, :].reshape(1, bp, CONV_W - 1, 2 * ML_COLS),
        per_head(qkvt_s[c:2 * c], 2 * DA_QK_DIM).reshape(1, bs, ts, DA_HEADS, 2 * DA_QK_DIM),
        per_head(qkvt_s[2 * c:3 * c], DA_V_DIM).reshape(1, bs, ts, DA_HEADS, DA_V_DIM),
        jnp.transpose(c_s, (3, 0, 1, 2)).reshape(1, bs, ML_HEADS, hd, hd),
        jnp.transpose(n_s, (2, 0, 1)).reshape(1, bs, ML_HEADS, hd),
        m_s.reshape(ML_HEADS, bs).T.reshape(1, bs, ML_HEADS),
        jnp.transpose(conv_new_s, (1, 0, 2)).reshape(1, bs, CONV_W - 1, 2 * ML_COLS),
    )
```

```python
import functools
import math

import jax
import jax.numpy as jnp
import numpy as np
from jax import lax
from jax.experimental import pallas as pl
from jax.experimental.pallas import tpu as pltpu

F32 = jnp.float32
BF16 = jnp.bfloat16
I32 = jnp.int32
HI = lax.Precision.HIGHEST

DA_HEADS = 8
DA_QK_DIM = 32
DA_V_DIM = 64
ML_HEADS = 8
ML_HEAD_DIM = 64
CONV_W = 4
NUM_BUCKETS = 32
MAX_DISTANCE = 128
PAGE_SIZE = 128
N_GROUPS = 8
TOPK_GROUPS = 4
TOP_K = 8
ROUTED_SCALE = 2.5
EPS = 1e-6

DA_COLS = DA_HEADS * 2 * DA_QK_DIM
ML_COLS = ML_HEADS * ML_HEAD_DIM
N_GATES = 2 * ML_HEADS
Q_SCALE = DA_QK_DIM ** -0.5
K_SCALE = ML_HEAD_DIM ** -0.5
LOG2E = 1.0 / math.log(2.0)
NEG = -1e30

LANES = 128
SUBLANES = 8
VMEM_LIMIT = 56 * 1024 * 1024

ML_CHUNK = 128
ATT_TILE = 1024
ATT_SUB = 512
MOE_BM = 256
TOK_BLK = 128
PAGES_PER_STEP = 32
PROJ_TM = 256

NT = (((1,), (1,)), ((), ()))


def _const_spec(shape):
    nd = len(shape)
    return pl.BlockSpec(shape, lambda *_: (0,) * nd)


def _params(sem, vmem=VMEM_LIMIT):
    return pltpu.CompilerParams(dimension_semantics=sem, vmem_limit_bytes=vmem)


def _sigmoid(x):
    return 1.0 / (1.0 + jnp.exp(-x))


def _log_sigmoid(x):
    return jnp.minimum(x, 0.0) - jnp.log1p(jnp.exp(-jnp.abs(x)))


def _lam(lq1, lk1, lq2, lk2, lam_init):
    a = jnp.sum(lq1[...] * lk1[...], axis=-1, keepdims=True)
    b = jnp.sum(lq2[...] * lk2[...], axis=-1, keepdims=True)
    return jnp.exp(a) - jnp.exp(b) + lam_init


def _modulated_norm(x, g, shift, scale):
    y = x * lax.rsqrt(jnp.mean(x * x, axis=-1, keepdims=True) + EPS) * g
    return y * (1.0 + scale) + shift


def _t5_bucket(dist):
    n = jnp.maximum(dist, 0)
    max_exact = NUM_BUCKETS // 2
    nf = jnp.maximum(n, 1).astype(F32)
    large = max_exact + (jnp.log(nf / max_exact) / math.log(MAX_DISTANCE / max_exact)
                         * (NUM_BUCKETS - max_exact)).astype(I32)
    return jnp.where(n < max_exact, n, jnp.minimum(large, NUM_BUCKETS - 1))


def _bias_by_distance(rel_bias, n):
    rb = rel_bias.astype(F32)
    return rb[_t5_bucket(jnp.arange(n))] - rb[NUM_BUCKETS - 1][None, :]


def _adaln_kernel(c_ref, w_ref, b_ref, o_ref):
    c = c_ref[...]
    a = c * _sigmoid(c)
    o_ref[...] = jnp.dot(a, w_ref[...], precision=HI, preferred_element_type=F32) + b_ref[...]


def _adaln(c_all, w, b):
    r, d = c_all.shape
    n = w.shape[1]
    tn = n // 4
    return pl.pallas_call(
        _adaln_kernel,
        grid=(n // tn,),
        in_specs=[_const_spec((r, d)), pl.BlockSpec((d, tn), lambda j: (0, j)),
                  pl.BlockSpec((1, tn), lambda j: (0, j))],
        out_specs=pl.BlockSpec((r, tn), lambda j: (0, j)),
        out_shape=jax.ShapeDtypeStruct((r, n), F32),
        compiler_params=_params(("parallel",)),
        name="adaln",
    )(c_all, w, b.reshape(1, n))


def _inproj_kernel(x_ref, sh_ref, sc_ref, g_ref, wqkv_ref, wml_ref, wg_ref, bg_ref, bgt_ref,
                   qtb_ref, krb_ref, ktf_ref, vtf_ref, vtb_ref, mqk_ref, mv_ref, mo_ref, gates_ref, gatest_ref):
    h = _modulated_norm(x_ref[...], g_ref[...], sh_ref[...], sc_ref[...])
    hb = h.astype(BF16)
    c = DA_COLS
    qkvt = lax.dot_general(wqkv_ref[...], hb, NT, preferred_element_type=F32)
    qtb_ref[...] = (qkvt[0:c, :] * (Q_SCALE * LOG2E)).astype(BF16)
    ktf_ref[...] = qkvt[c:2 * c, :]
    vtf_ref[...] = qkvt[2 * c:3 * c, :]
    vtb_ref[...] = qkvt[2 * c:3 * c, :].astype(BF16)
    krb_ref[...] = lax.dot_general(hb, wqkv_ref[c:2 * c, :], NT, preferred_element_type=F32).astype(BF16)
    ml = lax.dot_general(hb, wml_ref[...], NT, preferred_element_type=F32)
    mqk_ref[...] = ml[:, 0:2 * ML_COLS]
    mv_ref[...] = ml[:, 2 * ML_COLS:3 * ML_COLS].astype(BF16)
    mo_ref[...] = ml[:, 3 * ML_COLS:4 * ML_COLS]
    graw = lax.dot_general(h, wg_ref[...], NT, precision=HI, preferred_element_type=F32) + bg_ref[...]
    lane = lax.broadcasted_iota(I32, graw.shape, 1)
    gates_ref[...] = jnp.where(lane < ML_HEADS, graw, _log_sigmoid(graw))
    grawt = lax.dot_general(wg_ref[...], h, NT, precision=HI, preferred_element_type=F32) + bgt_ref[...]
    sub = lax.broadcasted_iota(I32, grawt.shape, 0)
    gatest_ref[...] = jnp.where(sub < ML_HEADS, grawt, _log_sigmoid(grawt))


def _inproj(x, shift, scale, g, wqkv_t, wml_t, wg_t, b_gate):
    r, d = x.shape
    tm = min(PROJ_TM, r)
    row = lambda n: pl.BlockSpec((tm, n), lambda i: (i, 0))
    col = lambda n: pl.BlockSpec((n, tm), lambda i: (0, i))
    ng = N_GATES
    out_shape = [
        jax.ShapeDtypeStruct((DA_COLS, r), BF16), jax.ShapeDtypeStruct((r, DA_COLS), BF16),
        jax.ShapeDtypeStruct((DA_COLS, r), F32),
        jax.ShapeDtypeStruct((DA_COLS, r), F32), jax.ShapeDtypeStruct((DA_COLS, r), BF16),
        jax.ShapeDtypeStruct((r, 2 * ML_COLS), F32), jax.ShapeDtypeStruct((r, ML_COLS), BF16),
        jax.ShapeDtypeStruct((r, ML_COLS), F32),
        jax.ShapeDtypeStruct((r, ng), F32), jax.ShapeDtypeStruct((ng, r), F32),
    ]
    out_specs = [col(DA_COLS), row(DA_COLS), col(DA_COLS), col(DA_COLS), col(DA_COLS), row(2 * ML_COLS),
                 row(ML_COLS), row(ML_COLS), row(ng), col(ng)]
    return pl.pallas_call(
        _inproj_kernel,
        grid=(r // tm,),
        in_specs=[row(d), _const_spec((1, d)), _const_spec((1, d)), _const_spec((1, d)),
                  _const_spec(wqkv_t.shape), _const_spec(wml_t.shape),
                  _const_spec(wg_t.shape), _const_spec((1, ng)), _const_spec((ng, 1))],
        out_specs=out_specs,
        out_shape=out_shape,
        compiler_params=_params(("parallel",)),
        name="inproj",
    )(x, shift, scale, g, wqkv_t, wml_t, wg_t, b_gate.reshape(1, ng), b_gate.reshape(ng, 1))


def _pair_rmsnorm(x, lane, nd):
    sq = x * x
    lo = jnp.sum(jnp.where(lane < nd, sq, 0.0), axis=-1, keepdims=True) * (1.0 / nd)
    hi = jnp.sum(jnp.where(lane >= nd, sq, 0.0), axis=-1, keepdims=True) * (1.0 / nd)
    return x * jnp.where(lane < nd, lax.rsqrt(lo + EPS), lax.rsqrt(hi + EPS))


def _mlstm_kernel(mqk_ref, mv_ref, og_ref, gates_ref, gatest_ref, cw_ref, cb_ref, mlg_ref,
                  tri_ref, trit_ref, mo_ref, sout_ref, mout_ref, xprev, st, msc):
    c = pl.program_id(0)
    L = mqk_ref.shape[0]
    hd = ML_HEAD_DIM

    @pl.when(c == 0)
    def _():
        xprev[...] = jnp.zeros_like(xprev)
        st[...] = jnp.zeros_like(st)
        msc[...] = jnp.zeros_like(msc)

    x = mqk_ref[...]
    xp = xprev[...]
    row = lax.broadcasted_iota(I32, x.shape, 0)

    def shifted(j):
        return jnp.where(row < j, pltpu.roll(xp, j, 0), pltpu.roll(x, j, 0))

    y = cb_ref[...] + shifted(3) * cw_ref[0:1, :]
    y = y + shifted(2) * cw_ref[1:2, :]
    y = y + shifted(1) * cw_ref[2:3, :]
    y = y + x * cw_ref[3:4, :]
    qk = y * _sigmoid(y)
    xprev[...] = x

    g = gates_ref[...]
    gt = gatest_ref[...]
    fcol_all = jnp.dot(tri_ref[...], g, precision=HI, preferred_element_type=F32)
    frow_all = jnp.dot(gt, trit_ref[...], precision=HI, preferred_element_type=F32)
    r_i = lax.broadcasted_iota(I32, (L, L), 0)
    c_i = lax.broadcasted_iota(I32, (L, L), 1)
    causal = c_i <= r_i
    lane = lax.broadcasted_iota(I32, (L, 2 * hd), 1)

    for p in range(ML_HEADS // 2):
        qpair = qk[:, 2 * hd * p:2 * hd * (p + 1)]
        kpair = qk[:, ML_COLS + 2 * hd * p:ML_COLS + 2 * hd * (p + 1)] * K_SCALE
        vpair = mv_ref[:, 2 * hd * p:2 * hd * (p + 1)].astype(F32)
        hh, decays, vaugs, carries = [], [], [], []
        for j in range(2):
            h = 2 * p + j
            q_h = qpair[:, hd * j:hd * (j + 1)].astype(BF16)
            k_h = kpair[:, hd * j:hd * (j + 1)].astype(BF16)
            s_raw = lax.dot_general(q_h, k_h, NT, preferred_element_type=F32)
            fc = fcol_all[:, ML_HEADS + h:ML_HEADS + h + 1]
            fr = frow_all[ML_HEADS + h:ML_HEADS + h + 1, :]
            igr = gt[h:h + 1, :]
            igc = g[:, h:h + 1]
            m_old = msc[h]
            dlog = jnp.where(causal, fc - fr + igr, NEG)
            inter = m_old + fc
            m_t = jnp.maximum(inter, jnp.max(dlog, axis=-1, keepdims=True))
            dmat = jnp.where(causal, jnp.exp(dlog - m_t), 0.0)
            smat = (s_raw * dmat).astype(BF16)
            vsrc = vpair if j == 0 else pltpu.roll(vpair, hd, 1)
            vaug = jnp.where(lane < hd, vsrc, jnp.where(lane == hd, 1.0, 0.0)).astype(BF16)
            s_h = st[h]
            q_s = jnp.dot(q_h, s_h.astype(BF16), preferred_element_type=F32)
            num = jnp.dot(smat, vaug, preferred_element_type=F32) + jnp.exp(inter - m_t) * q_s
            den = num[:, hd:hd + 1]
            hh.append(num / jnp.maximum(jnp.abs(den), jnp.exp(-m_t)))
            m_new = m_t[L - 1:L, :]
            f_last = fc[L - 1:L, :]
            decays.append(jnp.exp(f_last - fc + igc - m_new))
            carries.append(jnp.exp(m_old + f_last - m_new))
            vaugs.append(vaug)
            msc[h] = m_new
        kd = kpair * jnp.where(lane < hd, decays[0], decays[1])
        kdt = kd.T
        for j in range(2):
            h = 2 * p + j
            upd = jnp.dot(kdt[hd * j:hd * (j + 1), :].astype(BF16), vaugs[j], preferred_element_type=F32)
            st[h] = carries[j] * st[h] + upd
        hpair = jnp.where(lane < hd, hh[0], pltpu.roll(hh[1], hd, 1))
        hn = _pair_rmsnorm(hpair, lane, hd) * mlg_ref[:, 2 * hd * p:2 * hd * (p + 1)]
        og = _sigmoid(og_ref[:, 2 * hd * p:2 * hd * (p + 1)])
        mo_ref[:, 2 * hd * p:2 * hd * (p + 1)] = (hn * og).astype(BF16)

    sout_ref[...] = st[...]
    mout_ref[...] = msc[...]


def _mlstm_prompt(mqk, mv, og, gates, gates_t, conv_w, conv_b, ml_g):
    t = mqk.shape[0]
    L = min(ML_CHUNK, t)
    hd = ML_HEAD_DIM
    tri = jnp.asarray(np.tril(np.ones((L, L), np.float32)))
    row = lambda n: pl.BlockSpec((L, n), lambda i: (i, 0))
    ng = N_GATES
    return pl.pallas_call(
        _mlstm_kernel,
        grid=(t // L,),
        in_specs=[row(2 * ML_COLS), row(ML_COLS), row(ML_COLS), row(ng),
                  pl.BlockSpec((ng, L), lambda i: (0, i)),
                  _const_spec((CONV_W, 2 * ML_COLS)), _const_spec((1, 2 * ML_COLS)),
                  _const_spec((1, ML_COLS)), _const_spec((L, L)), _const_spec((L, L))],
        out_specs=[row(ML_COLS), _const_spec((ML_HEADS, hd, 2 * hd)), _const_spec((ML_HEADS, 1, 1))],
        out_shape=[jax.ShapeDtypeStruct((t, ML_COLS), BF16),
                   jax.ShapeDtypeStruct((ML_HEADS, hd, 2 * hd), F32),
                   jax.ShapeDtypeStruct((ML_HEADS, 1, 1), F32)],
        scratch_shapes=[pltpu.VMEM((L, 2 * ML_COLS), F32), pltpu.VMEM((ML_HEADS, hd, 2 * hd), F32),
                        pltpu.VMEM((ML_HEADS, 1, 1), F32)],
        compiler_params=_params(("arbitrary",)),
        name="mlstm_prompt",
    )(mqk, mv, og, gates, gates_t, conv_w, conv_b.reshape(1, -1), ml_g.reshape(1, -1), tri, tri.T)


def _pattn_kernel(ii_ref, jj_ref, qt_ref, k_ref, vt_ref, bias_ref, dag_ref, lq1, lk1, lq2, lk2,
                  o_ref, m_sc, acc_sc, *, lam_init):
    s_id = pl.program_id(1)
    i = ii_ref[s_id]
    j = jj_ref[s_id]
    dq = DA_QK_DIM
    dv = DA_V_DIM
    den_row = (dv, 0)

    @pl.when(j == 0)
    def _():
        m_sc[...] = jnp.full_like(m_sc, NEG)
        acc_sc[...] = jnp.zeros_like(acc_sc)

    def step(use_bias):
        qt = qt_ref[...]
        k = k_ref[...]
        vt = vt_ref[...].astype(F32)
        sub_v = lax.broadcasted_iota(I32, vt.shape, 0)
        ones_row = ((sub_v == den_row[0]).astype(F32), (sub_v == den_row[1]).astype(F32))
        vt_aug = (jnp.where(sub_v < dv, vt, ones_row[0]).astype(BF16),
                  jnp.where(sub_v >= dv, vt, ones_row[1]).astype(BF16))
        tk = k.shape[0]
        sub = min(ATT_SUB, tk)
        work = [(idx, k0) for k0 in range(0, tk, sub) for idx in range(4)]

        def scores(idx, k0):
            c0 = idx * dq
            return jnp.dot(k[k0:k0 + sub, c0:c0 + dq], qt[c0:c0 + dq, :], preferred_element_type=F32)

        st_next = scores(*work[0])
        for n, (idx, k0) in enumerate(work):
            st = st_next
            if n + 1 < len(work):
                st_next = scores(*work[n + 1])
            hp = idx // 2
            if use_bias:
                st = st + bias_ref[hp, k0:k0 + sub, :]
            m_prev = m_sc[idx]
            m_new = jnp.maximum(m_prev, jnp.max(st, axis=0, keepdims=True))
            p = jnp.exp2(st - m_new)
            alpha = jnp.exp2(m_prev - m_new)
            m_sc[idx] = m_new
            acc_sc[idx] = acc_sc[idx] * alpha + jnp.dot(vt_aug[hp][:, k0:k0 + sub], p.astype(BF16),
                                                        preferred_element_type=F32)

    near = j >= i - 1
    pl.when(near)(lambda: step(True))
    pl.when(jnp.logical_not(near))(lambda: step(False))

    @pl.when(j == i)
    def _():
        lam = _lam(lq1, lk1, lq2, lk2, lam_init)
        for hp in range(2):
            a1 = acc_sc[2 * hp]
            a2 = acc_sc[2 * hp + 1]
            dr = den_row[hp]
            o = a1 / a1[dr:dr + 1, :] - lam * (a2 / a2[dr:dr + 1, :])
            o = o[hp * dv:(hp + 1) * dv, :]
            o = o * lax.rsqrt(jnp.mean(o * o, axis=0, keepdims=True) + EPS)
            o_ref[hp * dv:(hp + 1) * dv, :] = o * dag_ref[hp * dv:(hp + 1) * dv, :] * (1.0 - lam_init)


def _prompt_bias_tiles(rel_bias, tile):
    rb = rel_bias.astype(F32)
    rb = (rb - rb[NUM_BUCKETS - 1][None, :]) * LOG2E
    kpos = jnp.arange(tile)[:, None]
    qpos = jnp.arange(tile)[None, :]
    dist = jnp.stack([tile + qpos - kpos, qpos - kpos], axis=0)
    bucket = _t5_bucket(dist)[None]
    tiles = jnp.zeros((DA_HEADS, 2, tile, tile), F32)
    for b in range(NUM_BUCKETS - 1):
        tiles = jnp.where(bucket == b, rb[b][:, None, None, None], tiles)
    tiles = jnp.where((dist >= 0)[None], tiles, NEG)
    return tiles.reshape(DA_HEADS // 2, 2, 2, tile, tile)


def _prompt_attention(qt_bf, k_bf, vt_bf, rel_bias, da_g, lams, lam_init):
    t = k_bf.shape[0]
    tile = min(ATT_TILE, t)
    assert tile > MAX_DISTANCE or tile == t
    nq = t // tile
    ii, jj = [], []
    for i in range(nq):
        for j in range(i + 1):
            ii.append(i)
            jj.append(j)
    ii = jnp.asarray(np.array(ii, np.int32))
    jj = jnp.asarray(np.array(jj, np.int32))
    bias = _prompt_bias_tiles(rel_bias, tile)
    w = 2 * DA_V_DIM
    lam_specs = [_const_spec((1, DA_QK_DIM))] * 4
    grid_spec = pltpu.PrefetchScalarGridSpec(
        num_scalar_prefetch=2,
        grid=(DA_HEADS // 2, ii.shape[0]),
        in_specs=[
            pl.BlockSpec((w, tile), lambda p, s, ii, jj: (p, ii[s])),
            pl.BlockSpec((tile, w), lambda p, s, ii, jj: (jj[s], p)),
            pl.BlockSpec((w, tile), lambda p, s, ii, jj: (p, jj[s])),
            pl.BlockSpec((None, 2, None, tile, tile),
                         lambda p, s, ii, jj: (p, 0, jnp.clip(jj[s] - ii[s] + 1, 0, 1), 0, 0)),
            pl.BlockSpec((w, 1), lambda p, s, ii, jj: (p, 0)),
        ] + lam_specs,
        out_specs=pl.BlockSpec((w, tile), lambda p, s, ii, jj: (p, ii[s])),
        scratch_shapes=[pltpu.VMEM((4, 1, tile), F32), pltpu.VMEM((4, w, tile), F32)],
    )
    return pl.pallas_call(
        functools.partial(_pattn_kernel, lam_init=lam_init),
        grid_spec=grid_spec,
        out_shape=jax.ShapeDtypeStruct((DA_COLS, t), F32),
        compiler_params=_params(("parallel", "arbitrary")),
        name="prompt_attention",
    )(ii, jj, qt_bf, k_bf, vt_bf, bias, da_g.reshape(-1, 1), *lams)


def _sinproj_kernel(x_ref, sh_ref, sc_ref, g_ref, wqkv_ref, wml_ref, wg_ref, bgt_ref,
                    qkvt_ref, mqk_ref, mvot_ref, gatest_ref):
    h = _modulated_norm(x_ref[...], g_ref[...], sh_ref[...], sc_ref[...])
    hb = h.astype(BF16)
    qkvt_ref[...] = lax.dot_general(wqkv_ref[...], hb, NT, preferred_element_type=F32)
    mqk_ref[...] = lax.dot_general(hb, wml_ref[0:2 * ML_COLS, :], NT, preferred_element_type=F32)
    mvot_ref[...] = lax.dot_general(wml_ref[2 * ML_COLS:4 * ML_COLS, :], hb, NT, preferred_element_type=F32)
    grawt = lax.dot_general(wg_ref[...], h, NT, precision=HI, preferred_element_type=F32) + bgt_ref[...]
    sub = lax.broadcasted_iota(I32, grawt.shape, 0)
    gatest_ref[...] = jnp.where(sub < ML_HEADS, grawt, _log_sigmoid(grawt))


def _sample_inproj(x, shift, scale, g, wqkv_t, wml_t, wg_t, b_gate):
    b, d = x.shape
    return pl.pallas_call(
        _sinproj_kernel,
        out_shape=[jax.ShapeDtypeStruct((3 * DA_COLS, b), F32), jax.ShapeDtypeStruct((b, 2 * ML_COLS), F32),
                   jax.ShapeDtypeStruct((2 * ML_COLS, b), F32), jax.ShapeDtypeStruct((N_GATES, b), F32)],
        compiler_params=pltpu.CompilerParams(vmem_limit_bytes=VMEM_LIMIT),
        name="sample_inproj",
    )(x, shift, scale, g, wqkv_t, wml_t, wg_t, b_gate.reshape(N_GATES, 1))


def _sconv_kernel(mqk_ref, buf_ref, cw_ref, cb_ref, qkt_ref, new_ref):
    x = mqk_ref[...]
    y = cb_ref[...] + buf_ref[0] * cw_ref[0:1, :]
    y = y + buf_ref[1] * cw_ref[1:2, :]
    y = y + buf_ref[2] * cw_ref[2:3, :]
    y = y + x * cw_ref[3:4, :]
    a = y * _sigmoid(y)
    lane = lax.broadcasted_iota(I32, a.shape, 1)
    qkt_ref[...] = jnp.where(lane < ML_COLS, a, a * K_SCALE).T
    new_ref[0] = buf_ref[1]
    new_ref[1] = buf_ref[2]
    new_ref[2] = x


def _sample_conv(mqk, conv_buf, conv_w, conv_b):
    b, n = mqk.shape
    return pl.pallas_call(
        _sconv_kernel,
        out_shape=[jax.ShapeDtypeStruct((n, b), F32), jax.ShapeDtypeStruct(conv_buf.shape, F32)],
        compiler_params=pltpu.CompilerParams(vmem_limit_bytes=VMEM_LIMIT),
        name="sample_conv",
    )(mqk, conv_buf, conv_w, conv_b.reshape(1, -1))


def _smlstm_kernel(q_ref, k_ref, v_ref, og_ref, ig_ref, lf_ref, m_ref, c_ref, n_ref, mlg_ref,
                   co_ref, no_ref, mo_ref, ho_ref):
    hd = ML_HEAD_DIM
    ig = ig_ref[...]
    lf = lf_ref[...]
    m_old = m_ref[...]
    m_t = jnp.maximum(m_old + lf, ig)
    a = jnp.exp(ig - m_t)
    w = jnp.exp(m_old + lf - m_t)
    v = v_ref[...]

    def body(d, num):
        cn = w * c_ref[d] + (a * k_ref[pl.ds(d, 1), :]) * v
        co_ref[d] = cn
        return num + q_ref[pl.ds(d, 1), :] * cn

    num = lax.fori_loop(0, hd, body, jnp.zeros(v.shape, F32))
    nn = w * n_ref[...] + a * k_ref[...]
    den = jnp.sum(q_ref[...] * nn, axis=0, keepdims=True)
    hv = num / jnp.maximum(jnp.abs(den), jnp.exp(-m_t))
    hn = hv * lax.rsqrt(jnp.mean(hv * hv, axis=0, keepdims=True) + EPS) * mlg_ref[...]
    no_ref[...] = nn
    mo_ref[...] = m_t
    ho_ref[...] = hn * _sigmoid(og_ref[...])


def _sample_mlstm(qk_t, v_t, og_t, gates_t, m_t, c_t, n_t, ml_g):
    b = qk_t.shape[1]
    h, d = ML_HEADS, ML_HEAD_DIM
    head_rows = lambda off: pl.BlockSpec((d, b), lambda i: (i + off, 0))
    vec = pl.BlockSpec((None, 1, b), lambda i: (i, 0, 0))
    return pl.pallas_call(
        _smlstm_kernel,
        grid=(h,),
        in_specs=[head_rows(0), head_rows(h), head_rows(0), head_rows(0),
                  vec, pl.BlockSpec((None, 1, b), lambda i: (i + h, 0, 0)), vec,
                  pl.BlockSpec((None, d, d, b), lambda i: (i, 0, 0, 0)),
                  pl.BlockSpec((None, d, b), lambda i: (i, 0, 0)),
                  pl.BlockSpec((None, d, 1), lambda i: (i, 0, 0))],
        out_specs=[pl.BlockSpec((None, d, d, b), lambda i: (i, 0, 0, 0)),
                   pl.BlockSpec((None, d, b), lambda i: (i, 0, 0)), vec, head_rows(0)],
        out_shape=[jax.ShapeDtypeStruct((h, d, d, b), F32), jax.ShapeDtypeStruct((h, d, b), F32),
                   jax.ShapeDtypeStruct((h, 1, b), F32), jax.ShapeDtypeStruct((h * d, b), F32)],
        compiler_params=_params(("parallel",)),
        name="sample_mlstm",
    )(qk_t, qk_t, v_t, og_t, gates_t.reshape(2 * h, 1, b), gates_t.reshape(2 * h, 1, b),
      m_t.reshape(h, 1, b), c_t, n_t, ml_g.reshape(h, d, 1))


def _sattn_kernel(pt_ref, qt_ref, knt_ref, vnt_ref, bpage_ref, bnew_ref, dag_ref, lq1, lk1, lq2, lk2,
                  ck_ref, cv_ref, o_ref, s_sc, a_sc, acc, qb, kbuf, vbuf, ksem, vsem, *, n_pg, lam_init):
    b = pl.program_id(0)
    ph = pl.program_id(1)
    g = pl.program_id(2)
    n_b = pl.num_programs(0)
    n_g = pl.num_programs(2)
    dq = DA_QK_DIM
    dv = DA_V_DIM
    nh = DA_HEADS
    n_pages = s_sc.shape[0] - 1
    lane = lax.broadcasted_iota(I32, qt_ref.shape, 1)

    def group_copies(cache_ref, buf, sem, bq, gq, slot):
        return [pltpu.make_async_copy(cache_ref.at[pt_ref[gq * n_pg + i, bq]], buf.at[slot, i], sem.at[slot])
                for i in range(n_pg)]

    def start_group(phq, bq, gq):
        slot = (bq * n_g + gq) % 2

        @pl.when(phq == 0)
        def _():
            for cp in group_copies(ck_ref, kbuf, ksem, bq, gq, slot):
                cp.start()

        @pl.when(phq == 1)
        def _():
            for cp in group_copies(cv_ref, vbuf, vsem, bq, gq, slot):
                cp.start()

    step = (b * 2 + ph) * n_g + g
    slot = (b * n_g + g) % 2

    @pl.when(step == 0)
    def _():
        start_group(ph, b, g)

    @pl.when(step + 1 < n_b * 2 * n_g)
    def _():
        nxt = step + 1
        start_group((nxt // n_g) % 2, nxt // (2 * n_g), nxt % n_g)

    k_refs = [kbuf.at[slot, i] for i in range(n_pg)]
    v_refs = [vbuf.at[slot, i] for i in range(n_pg)]

    def column(ref):
        return jnp.sum(jnp.where(lane == b, ref[...], 0.0), axis=1, keepdims=True)

    @pl.when((b == 0) & (ph == 0) & (g == 0))
    def _():
        o_ref[...] = jnp.zeros_like(o_ref)

    @pl.when((ph == 0) & (g == 0))
    def _():
        qb[...] = jnp.broadcast_to(column(qt_ref) * Q_SCALE, qb.shape)

    @pl.when(ph == 0)
    def _():
        for cp in group_copies(ck_ref, kbuf, ksem, b, g, slot):
            cp.wait()
        for c in range(2 * nh):
            qc = qb[c * dq:(c + 1) * dq, :]
            srow = (c % 2) * nh + c // 2
            for i in range(n_pg):
                prod = k_refs[i][c * dq:(c + 1) * dq, :] * qc
                s_sc[g * n_pg + i, srow:srow + 1, :] = jnp.sum(prod, axis=0, keepdims=True)

        @pl.when(g == n_g - 1)
        def _():
            s_sc[n_pages - 1] = s_sc[n_pages - 1] + bpage_ref[...]
            prodn = column(knt_ref) * qb[:, 0:1]
            sub = lax.broadcasted_iota(I32, (2 * nh, 1), 0)
            ln = bnew_ref[...]
            for r in range(2 * nh):
                c = (r % nh) * 2 + r // nh
                ln = ln + jnp.where(sub == r, jnp.sum(prodn[c * dq:(c + 1) * dq, :], axis=0, keepdims=True), 0.0)
            lane_s = lax.broadcasted_iota(I32, (2 * nh, LANES), 1)
            s_sc[n_pages] = jnp.where(lane_s == 0, ln, NEG)
            s_all = s_sc[...]
            mx = jnp.max(jnp.max(s_all, axis=0), axis=1, keepdims=True)
            p = jnp.exp(s_all - mx)
            l = jnp.sum(jnp.sum(p, axis=0), axis=1, keepdims=True)
            lam = _lam(lq1, lk1, lq2, lk2, lam_init)
            a_sc[...] = p[:, 0:nh, :] / l[0:nh, :] - lam * (p[:, nh:2 * nh, :] / l[nh:2 * nh, :])
            acc[...] = jnp.zeros_like(acc)

    @pl.when(ph == 1)
    def _():
        for cp in group_copies(cv_ref, vbuf, vsem, b, g, slot):
            cp.wait()
        for h in range(nh):
            part = acc[h * dv:(h + 1) * dv, :]
            for i in range(n_pg):
                part = part + v_refs[i][h * dv:(h + 1) * dv, :] * a_sc[g * n_pg + i, h:h + 1, :]
            acc[h * dv:(h + 1) * dv, :] = part

        @pl.when(g == n_g - 1)
        def _():
            out = jnp.sum(acc[...], axis=1, keepdims=True)
            vn = column(vnt_ref)
            a_new = a_sc[n_pages][:, 0:1]
            pieces = []
            for h in range(nh):
                oh = out[h * dv:(h + 1) * dv, :] + a_new[h:h + 1, :] * vn[h * dv:(h + 1) * dv, :]
                pieces.append(oh * lax.rsqrt(jnp.mean(oh * oh, axis=0, keepdims=True) + EPS))
            res = jnp.concatenate(pieces, axis=0) * dag_ref[...] * (1.0 - lam_init)
            o_ref[...] = jnp.where(lane == b, res, o_ref[...])


def _sample_attention(q_t, kn_t, vn_t, ck_t, cv_t, page_table_t, rel_bias, da_g, lams, lam_init):
    w, b = q_t.shape
    n_pages = page_table_t.shape[0]
    n_pg = min(PAGES_PER_STEP, n_pages)
    n_g = n_pages // n_pg
    ps = PAGE_SIZE
    past = n_pages * ps
    nh = DA_HEADS
    tab = _bias_by_distance(rel_bias, ps + 1)
    bpage = jnp.tile(tab[ps - jnp.arange(ps)].T, (2, 1))
    bnew = jnp.tile(tab[0:1].T, (2, 1))

    in_specs = [_const_spec((w, b))] * 3 + [_const_spec((2 * nh, ps)), _const_spec((2 * nh, 1)),
                                            _const_spec((w, 1))] + [_const_spec((1, DA_QK_DIM))] * 4
    in_specs += [pl.BlockSpec(memory_space=pl.ANY)] * 2
    grid_spec = pltpu.PrefetchScalarGridSpec(
        num_scalar_prefetch=1,
        grid=(b, 2, n_g),
        in_specs=in_specs,
        out_specs=_const_spec((w, b)),
        scratch_shapes=[pltpu.VMEM((n_pages + 1, 2 * nh, ps), F32), pltpu.VMEM((n_pages + 1, nh, ps), F32),
                        pltpu.VMEM((w, ps), F32), pltpu.VMEM((w, ps), F32),
                        pltpu.VMEM((2, n_pg, w, ps), F32), pltpu.VMEM((2, n_pg, w, ps), F32),
                        pltpu.SemaphoreType.DMA((2,)), pltpu.SemaphoreType.DMA((2,))],
    )
    return pl.pallas_call(
        functools.partial(_sattn_kernel, n_pg=n_pg, lam_init=lam_init),
        grid_spec=grid_spec,
        out_shape=jax.ShapeDtypeStruct((w, b), F32),
        compiler_params=_params(("arbitrary", "arbitrary", "arbitrary")),
        name="sample_attention",
    )(page_table_t, q_t, kn_t, vn_t, bpage, bnew, da_g.reshape(w, 1), *lams, ck_t, cv_t)


def _outproj_kernel(ap_ref, mop_ref, xp_ref, g1p_ref, sh2p_ref, sc2p_ref, g2p_ref,
                    as_ref, mos_ref, xs_ref, g1s_ref, sh2s_ref, sc2s_ref, g2s_ref,
                    wo_ref, n2_ref, wrt_ref, wsg_ref, wsu_ref, wsd_ref,
                    x1p_ref, x1s_ref, h2t_ref, lt_ref):
    i = pl.program_id(0)
    n_prompt = pl.num_programs(0) - 1

    def block(a, mo, x, g1, sh2, sc2, g2, out_ref):
        half = a.shape[1]
        mix = jnp.dot(a, wo_ref[0:half, :], preferred_element_type=F32)
        mix = mix + jnp.dot(mo, wo_ref[half:2 * half, :], preferred_element_type=F32)
        x1 = x + g1 * mix
        h2 = _modulated_norm(x1, n2_ref[...], sh2, sc2)
        lt_ref[...] = lax.dot_general(wrt_ref[...], h2, NT, precision=HI, preferred_element_type=F32)
        tm = h2.shape[0]
        for s in range(h2.shape[1] // LANES):
            h2t_ref[pl.ds(s, tm, stride=SUBLANES), :] = h2[:, s * LANES:(s + 1) * LANES]
        hb = h2.astype(BF16)
        gate = jnp.dot(hb, wsg_ref[...], preferred_element_type=F32)
        up = jnp.dot(hb, wsu_ref[...], preferred_element_type=F32)
        act = (gate * _sigmoid(gate) * up).astype(BF16)
        shared = jnp.dot(act, wsd_ref[...], preferred_element_type=F32)
        out_ref[...] = x1 + g2 * shared

    @pl.when(i < n_prompt)
    def _():
        block(ap_ref[...].T.astype(BF16), mop_ref[...], xp_ref[...], g1p_ref[...], sh2p_ref[...], sc2p_ref[...],
              g2p_ref[...], x1p_ref)

    @pl.when(i == n_prompt)
    def _():
        block(as_ref[...].T.astype(BF16), mos_ref[...].T.astype(BF16), xs_ref[...], g1s_ref[...], sh2s_ref[...],
              sc2s_ref[...], g2s_ref[...], x1s_ref)


def _outproj(a_p, mo_p, x_p, mod_p, a_ts, mo_ts, x_s, mod_s, w_out, n2, w_router_t, wsg, wsu, wsd):
    t, d = x_p.shape
    bs = x_s.shape[0]
    tm = TOK_BLK
    assert bs == tm and t % tm == 0
    nbp = t // tm
    e = w_router_t.shape[0]
    nsl = d // LANES
    prow = lambda n: pl.BlockSpec((tm, n), lambda i: (jnp.minimum(i, nbp - 1), 0))
    pcol = pl.BlockSpec((DA_COLS, tm), lambda i: (0, jnp.minimum(i, nbp - 1)))
    in_specs = ([pcol, prow(ML_COLS), prow(d)] + [_const_spec((1, d))] * 4
                + [_const_spec((DA_COLS, bs)), _const_spec((ML_COLS, bs)), _const_spec((bs, d))]
                + [_const_spec((bs, d))] * 4
                + [_const_spec(w_out.shape), _const_spec((1, d)), _const_spec(w_router_t.shape),
                   _const_spec(wsg.shape), _const_spec(wsu.shape), _const_spec(wsd.shape)])
    return pl.pallas_call(
        _outproj_kernel,
        grid=(nbp + 1,),
        in_specs=in_specs,
        out_specs=[prow(d), _const_spec((bs, d)), pl.BlockSpec((tm * nsl, LANES), lambda i: (i, 0)),
                   pl.BlockSpec((e, tm), lambda i: (0, i))],
        out_shape=[jax.ShapeDtypeStruct((t, d), F32), jax.ShapeDtypeStruct((bs, d), F32),
                   jax.ShapeDtypeStruct(((t + bs) * nsl, LANES), F32), jax.ShapeDtypeStruct((e, t + bs), F32)],
        compiler_params=_params(("arbitrary",)),
        name="outproj",
    )(a_p, mo_p, x_p, *mod_p, a_ts, mo_ts, x_s, *mod_s, w_out, n2, w_router_t, wsg, wsu, wsd)


def _first_max(cur, iota, n, axis=0):
    t = jnp.max(cur, axis=axis, keepdims=True)
    idx = jnp.min(jnp.where(cur == t, iota, float(n)), axis=axis, keepdims=True)
    return t, idx


def _route_kernel(lt_ref, bias_ref, ut_ref, ones_ref, eidx_ref, w_ref, rank_ref, cnt_ref, carry):
    i = pl.program_id(0)

    @pl.when(i == 0)
    def _():
        carry[...] = jnp.zeros_like(carry)

    aff = _sigmoid(lt_ref[...])
    sel = aff + bias_ref[...]
    n_e, n_t = aff.shape
    gsz = n_e // N_GROUPS
    e_iota = lax.broadcasted_iota(I32, (n_e, n_t), 0).astype(F32)
    g_iota = lax.broadcasted_iota(I32, (N_GROUPS, n_t), 0).astype(F32)
    in_iota = lax.broadcasted_iota(I32, (gsz, n_t), 0).astype(F32)
    ninf = -jnp.inf
    gscore = jnp.zeros((N_GROUPS, n_t), F32)
    for g in range(N_GROUPS):
        blk = sel[g * gsz:(g + 1) * gsz, :]
        t1, i1 = _first_max(blk, in_iota, gsz)
        t2 = jnp.max(jnp.where(in_iota == i1, ninf, blk), axis=0, keepdims=True)
        gscore = jnp.where(g_iota == float(g), t1 + t2, gscore)
    gmask = jnp.zeros((N_GROUPS, n_t), F32)
    cur = gscore
    for _ in range(TOPK_GROUPS):
        _, gi = _first_max(cur, g_iota, N_GROUPS)
        hit = g_iota == gi
        gmask = jnp.where(hit, 1.0, gmask)
        cur = jnp.where(hit, ninf, cur)
    cur = jnp.concatenate(
        [jnp.where(gmask[g:g + 1, :] > 0.0, sel[g * gsz:(g + 1) * gsz, :], ninf) for g in range(N_GROUPS)], axis=0)
    hits, ws = [], []
    msum = jnp.zeros((n_e, n_t), F32)
    for k in range(TOP_K):
        _, ei = _first_max(cur, e_iota, n_e)
        hit = e_iota == ei
        ws.append(jnp.sum(jnp.where(hit, aff, 0.0), axis=0, keepdims=True))
        cur = jnp.where(hit, ninf, cur)
        msum = msum + hit.astype(F32)
        hits.append(hit)
        eidx_ref[k:k + 1, :] = ei.astype(I32)
    wsum = ws[0]
    for k in range(1, TOP_K):
        wsum = wsum + ws[k]
    mb = msum.astype(BF16)
    base = carry[...] + jnp.dot(mb, ut_ref[...], preferred_element_type=F32)
    for k in range(TOP_K):
        w_ref[k:k + 1, :] = ws[k] / wsum * ROUTED_SCALE
        rank_ref[k:k + 1, :] = jnp.sum(jnp.where(hits[k], base, 0.0), axis=0, keepdims=True).astype(I32)
    carry[...] = carry[...] + jnp.dot(mb, ones_ref[...], preferred_element_type=F32)
    cnt_ref[...] = carry[...]


def _route(logits_t, router_bias):
    n_e, t_all = logits_t.shape
    tb = TOK_BLK
    ut = jnp.asarray(np.triu(np.ones((tb, tb), np.float32), 1), dtype=BF16)
    ones = jnp.ones((tb, tb), BF16)
    bias_b = jnp.broadcast_to(router_bias.astype(F32)[:, None], (n_e, tb))
    tok = pl.BlockSpec((TOP_K, tb), lambda i: (0, i))
    return pl.pallas_call(
        _route_kernel,
        grid=(t_all // tb,),
        in_specs=[pl.BlockSpec((n_e, tb), lambda i: (0, i)), _const_spec((n_e, tb)), _const_spec((tb, tb)),
                  _const_spec((tb, tb))],
        out_specs=[tok, tok, tok, _const_spec((n_e, tb))],
        out_shape=[jax.ShapeDtypeStruct((TOP_K, t_all), I32), jax.ShapeDtypeStruct((TOP_K, t_all), F32),
                   jax.ShapeDtypeStruct((TOP_K, t_all), I32), jax.ShapeDtypeStruct((n_e, tb), F32)],
        scratch_shapes=[pltpu.VMEM((n_e, tb), F32)],
        compiler_params=_params(("arbitrary",)),
        name="moe_route",
    )(logits_t, bias_b, ut, ones)


def _pos_kernel(eidx_ref, rank_ref, offs_ref, pos_ref):
    offs = offs_ref[...]
    e_iota = lax.broadcasted_iota(I32, offs.shape, 0)
    for k in range(TOP_K):
        hit = e_iota == eidx_ref[k:k + 1, :]
        base = jnp.sum(jnp.where(hit, offs, 0.0), axis=0, keepdims=True).astype(I32)
        pos_ref[k:k + 1, :] = base + rank_ref[k:k + 1, :]


def _positions(eidx, rank, offs):
    t_all = eidx.shape[1]
    tb = TOK_BLK
    n_e = offs.shape[0]
    tok = pl.BlockSpec((TOP_K, tb), lambda i: (0, i))
    return pl.pallas_call(
        _pos_kernel,
        grid=(t_all // tb,),
        in_specs=[tok, tok, _const_spec((n_e, tb))],
        out_specs=tok,
        out_shape=jax.ShapeDtypeStruct((TOP_K, t_all), I32),
        compiler_params=_params(("parallel",)),
        name="moe_positions",
    )(eidx, rank, jnp.broadcast_to(offs.astype(F32)[:, None], (n_e, tb)))


def _tile_start(row):
    start = row * SUBLANES
    return start if isinstance(start, int) else pl.multiple_of(start, SUBLANES)


def _row_copy(src, src_row, dst, dst_row, sem):
    return pltpu.make_async_copy(src.at[pl.ds(_tile_start(src_row), SUBLANES)],
                                 dst.at[pl.ds(_tile_start(dst_row), SUBLANES)], sem)


def _dispatch_kernel(pos_ref, h2t_ref, xs_in_ref, xs_ref, sem):
    del xs_in_ref
    tb = pos_ref.shape[1]

    def issue(t, carry):
        for k in range(TOP_K):
            _row_copy(h2t_ref, t, xs_ref, pos_ref[k, t], sem).start(priority=k % 2)
        return carry

    lax.fori_loop(0, tb, issue, 0)

    def drain(t, carry):
        for k in range(TOP_K):
            _row_copy(h2t_ref, 0, xs_ref, 0, sem).wait()
        return carry

    lax.fori_loop(0, tb, drain, 0)


def _dispatch(pos, h2t, n_rows):
    t_all = pos.shape[1]
    tb = TOK_BLK
    xs0 = jnp.zeros((n_rows * SUBLANES, LANES), F32)
    return pl.pallas_call(
        _dispatch_kernel,
        grid=(t_all // tb,),
        in_specs=[pl.BlockSpec((TOP_K, tb), lambda i: (0, i), memory_space=pltpu.SMEM),
                  pl.BlockSpec((tb * SUBLANES, LANES), lambda i: (i, 0)), pl.BlockSpec(memory_space=pl.ANY)],
        out_specs=pl.BlockSpec(memory_space=pl.ANY),
        out_shape=jax.ShapeDtypeStruct(xs0.shape, F32),
        input_output_aliases={2: 0},
        scratch_shapes=[pltpu.SemaphoreType.DMA(())],
        compiler_params=_params(("arbitrary",)),
        name="moe_dispatch",
    )(pos, h2t, xs0)


def _experts_kernel(be_ref, nv_ref, xs_ref, wg_ref, wu_ref, wd_ref, ys_ref, wgb, wub, wdb):
    i = pl.program_id(0)

    @pl.when(i < nv_ref[0])
    def _():
        @pl.when((i == 0) | (be_ref[i] != be_ref[jnp.maximum(i - 1, 0)]))
        def _():
            wgb[...] = wg_ref[...].astype(BF16)
            wub[...] = wu_ref[...].astype(BF16)
            wdb[...] = wd_ref[...].astype(BF16)

        bm = xs_ref.shape[0] // SUBLANES
        nsl = wg_ref.shape[0] // LANES
        x = jnp.concatenate([xs_ref[pl.ds(s, bm, stride=SUBLANES), :] for s in range(nsl)], axis=1).astype(BF16)
        gate = jnp.dot(x, wgb[...], preferred_element_type=F32)
        up = jnp.dot(x, wub[...], preferred_element_type=F32)
        act = (gate * _sigmoid(gate) * up).astype(BF16)
        y = jnp.dot(act, wdb[...], preferred_element_type=F32)
        for s in range(nsl):
            ys_ref[pl.ds(s, bm, stride=SUBLANES), :] = y[:, s * LANES:(s + 1) * LANES]


def _experts(block_e, n_valid, xs, w_g, w_u, w_d, n_blocks):
    bm = MOE_BM
    d, de = w_g.shape[1], w_g.shape[2]
    rows = pl.BlockSpec((bm * SUBLANES, LANES), lambda i, be, nv: (jnp.minimum(i, nv[0] - 1), 0))
    grid_spec = pltpu.PrefetchScalarGridSpec(
        num_scalar_prefetch=2,
        grid=(n_blocks,),
        in_specs=[rows,
                  pl.BlockSpec((None, d, de), lambda i, be, nv: (be[i], 0, 0)),
                  pl.BlockSpec((None, d, de), lambda i, be, nv: (be[i], 0, 0)),
                  pl.BlockSpec((None, de, d), lambda i, be, nv: (be[i], 0, 0))],
        out_specs=rows,
        scratch_shapes=[pltpu.VMEM((d, de), BF16), pltpu.VMEM((d, de), BF16), pltpu.VMEM((de, d), BF16)],
    )
    return pl.pallas_call(
        _experts_kernel,
        grid_spec=grid_spec,
        out_shape=jax.ShapeDtypeStruct(xs.shape, F32),
        input_output_aliases={2: 0},
        compiler_params=_params(("arbitrary",)),
        name="moe_experts",
    )(block_e, n_valid, xs, w_g, w_u, w_d)


def _combine_kernel(pos_ref, w_ref, x1p_ref, g2p_ref, x1s_ref, g2s_ref, fg_ref, ys_ref, yp_ref, ysm_ref, buf, sem):
    i = pl.program_id(0)
    n_prompt = pl.num_programs(0) - 1
    tb = pos_ref.shape[1]
    nsl = x1p_ref.shape[1] // LANES

    def issue(t, carry):
        for k in range(TOP_K):
            pltpu.make_async_copy(ys_ref.at[pl.ds(_tile_start(pos_ref[k, t]), SUBLANES)],
                                  buf.at[k, pl.ds(_tile_start(t), SUBLANES)], sem).start(priority=k % 2)
        return carry

    lax.fori_loop(0, tb, issue, 0)

    def drain(t, carry):
        for k in range(TOP_K):
            pltpu.make_async_copy(ys_ref.at[pl.ds(0, SUBLANES)], buf.at[k, pl.ds(0, SUBLANES)], sem).wait()
        return carry

    lax.fori_loop(0, tb, drain, 0)

    w = w_ref[...]
    routed = None
    for k in range(TOP_K):
        yk = jnp.concatenate([buf[k, pl.ds(s, tb, stride=SUBLANES), :] for s in range(nsl)], axis=1)
        term = yk * w[:, k:k + 1]
        routed = term if routed is None else routed + term

    def finish(x1, g2, out_ref):
        x2 = x1 + g2 * routed
        out_ref[...] = x2 * lax.rsqrt(jnp.mean(x2 * x2, axis=-1, keepdims=True) + EPS) * fg_ref[...]

    @pl.when(i < n_prompt)
    def _():
        finish(x1p_ref[...], g2p_ref[...], yp_ref)

    @pl.when(i == n_prompt)
    def _():
        finish(x1s_ref[...], g2s_ref[...], ysm_ref)


def _combine(pos, w_tm, x1_p, g2_p, x1_s, g2_s, final_g, ys):
    t, d = x1_p.shape
    bs = x1_s.shape[0]
    tb = TOK_BLK
    assert bs == tb and t % tb == 0
    nbp = t // tb
    prow = pl.BlockSpec((tb, d), lambda i: (jnp.minimum(i, nbp - 1), 0))
    return pl.pallas_call(
        _combine_kernel,
        grid=(nbp + 1,),
        in_specs=[pl.BlockSpec((TOP_K, tb), lambda i: (0, i), memory_space=pltpu.SMEM),
                  pl.BlockSpec((tb, TOP_K), lambda i: (i, 0)),
                  prow, _const_spec((1, d)), _const_spec((bs, d)), _const_spec((bs, d)), _const_spec((1, d)),
                  pl.BlockSpec(memory_space=pl.ANY)],
        out_specs=[prow, _const_spec((bs, d))],
        out_shape=[jax.ShapeDtypeStruct((t, d), F32), jax.ShapeDtypeStruct((bs, d), F32)],
        scratch_shapes=[pltpu.VMEM((TOP_K, tb * SUBLANES, LANES), F32), pltpu.SemaphoreType.DMA(())],
        compiler_params=_params(("arbitrary",)),
        name="moe_combine",
    )(pos, w_tm, x1_p, g2_p, x1_s, g2_s, final_g.reshape(1, d), ys)


def kernel(x_prompt, x_sample, cache_k, cache_v, state_C, state_n, state_m, state_conv, page_table, c_prompt, c_sample, w_ada, b_ada, norm1_g, norm2_g, w_in, conv_w, conv_b, lambda_q1, lambda_k1, lambda_q2, lambda_k2, da_norm_g, rel_bias, ml_b_i, ml_b_f, ml_norm_g, w_out, w_router, router_bias, w_exp_gate, w_exp_up, w_exp_down, w_sh_gate, w_sh_up, w_sh_down, final_norm_g):
    bp, t, d = x_prompt.shape
    bs, ts, _ = x_sample.shape
    depth = w_ada.shape[0]
    assert bp == 1 and ts == 1 and depth == 1
    assert d % LANES == 0 and (t + bs) % TOK_BLK == 0 and t % TOK_BLK == 0
    l = 0
    lam_init = 0.8 - 0.6 * math.exp(-0.3 * l)
    lams = [a[l].reshape(1, -1).astype(F32) for a in (lambda_q1, lambda_k1, lambda_q2, lambda_k2)]
    t_all = t + bs
    hd = ML_HEAD_DIM

    pad = (-(bp + bs)) % SUBLANES
    c_all = jnp.concatenate([c_prompt, c_sample, jnp.zeros((pad, d), F32)], axis=0)
    mod = _adaln(c_all, w_ada[l], b_ada[l])
    modp = [mod[0:1, i * d:(i + 1) * d] for i in range(6)]
    mods = [mod[1:1 + bs, i * d:(i + 1) * d] for i in range(6)]

    w_in_t = w_in[l].T
    c = DA_COLS
    wqkv_t = w_in_t[0:3 * c].astype(BF16)
    wml_t = w_in_t[3 * c:3 * c + 4 * ML_COLS].astype(BF16)
    wg_t = w_in_t[3 * c + 4 * ML_COLS:]
    b_gate = jnp.concatenate([ml_b_i[l], ml_b_f[l]]).astype(F32)
    w_out_b = w_out[l].astype(BF16)
    w_router_t = w_router[l].T
    wsg, wsu, wsd = w_sh_gate[l].astype(BF16), w_sh_up[l].astype(BF16), w_sh_down[l].astype(BF16)
    n1 = norm1_g[l].reshape(1, d)
    n2 = norm2_g[l].reshape(1, d)

    xp = x_prompt.reshape(t, d)
    (qtb, krb, ktf, vtf, vtb, mqk, mv, mo_raw, gates, gates_t) = _inproj(
        xp, modp[0], modp[1], n1, wqkv_t, wml_t, wg_t, b_gate)
    attn_p = _prompt_attention(qtb, krb, vtb, rel_bias, da_norm_g[l], lams, lam_init)
    mo_p, s_p, m_p = _mlstm_prompt(mqk, mv, mo_raw, gates, gates_t, conv_w[l], conv_b[l], ml_norm_g[l])

    xs = x_sample.reshape(bs, d)
    qkvt_s, mqk_s, mvot_s, gates_ts = _sample_inproj(xs, mods[0], mods[1], n1, wqkv_t, wml_t, wg_t, b_gate)
    conv_buf = jnp.transpose(state_conv[l], (1, 0, 2))
    qkt_s, conv_new_s = _sample_conv(mqk_s, conv_buf, conv_w[l], conv_b[l])
    c_t = jnp.transpose(state_C[l], (1, 2, 3, 0))
    n_t = jnp.transpose(state_n[l], (1, 2, 0))
    m_t = state_m[l].T
    c_s, n_s, m_s, mo_ts = _sample_mlstm(qkt_s, mvot_s[0:ML_COLS], mvot_s[ML_COLS:2 * ML_COLS], gates_ts, m_t, c_t,
                                         n_t, ml_norm_g[l])
    n_pool = cache_k.shape[1]
    ck_t = jnp.transpose(cache_k[l], (0, 2, 3, 1)).reshape(n_pool, c, PAGE_SIZE)
    cv_t = jnp.transpose(cache_v[l], (0, 2, 3, 1)).reshape(n_pool, c, PAGE_SIZE)
    attn_ts = _sample_attention(qkvt_s[0:c], qkvt_s[c:2 * c], qkvt_s[2 * c:3 * c], ck_t, cv_t,
                                page_table.T, rel_bias, da_norm_g[l], lams, lam_init)

    xs1_p, xs1_s, h2t, logits_t = _outproj(attn_p, mo_p, xp, modp[2:6], attn_ts, mo_ts, xs, mods[2:6], w_out_b, n2,
                                           w_router_t, wsg, wsu, wsd)

    n_e = w_router.shape[2]
    eidx, wts, rank, cnt = _route(logits_t, router_bias[l])
    counts = cnt[:, 0].astype(I32)
    padded = (counts + MOE_BM - 1) // MOE_BM * MOE_BM
    pad_end = jnp.cumsum(padded)
    offs = pad_end - padded
    n_blocks = -(-(t_all * TOP_K) // MOE_BM) + n_e
    block_e = jnp.minimum(jnp.searchsorted(pad_end, jnp.arange(n_blocks) * MOE_BM, side='right'),
                          n_e - 1).astype(I32)
    n_valid = (pad_end[-1:] // MOE_BM).astype(I32)
    pos = _positions(eidx, rank, offs)
    xs_sorted = _dispatch(pos, h2t, n_blocks * MOE_BM)
    ys = _experts(block_e, n_valid, xs_sorted, w_exp_gate[l], w_exp_up[l], w_exp_down[l], n_blocks)
    w_tm = wts.T
    y_p, y_s = _combine(pos, w_tm, xs1_p, modp[5], xs1_s, mods[5], final_norm_g, ys)

    def per_head(a_t, n_feat):
        return jnp.transpose(a_t.reshape(-1, n_feat, a_t.shape[1]), (2, 0, 1))

    return (
        y_p.reshape(bp, t, d),
        y_s.reshape(bs, ts, d),
        per_head(ktf, 2 * DA_QK_DIM).reshape(1, bp, t, DA_HEADS, 2 * DA_QK_DIM),
        per_head(vtf, DA_V_DIM).reshape(1, bp, t, DA_HEADS, DA_V_DIM),
        s_p[:, :, :hd].reshape(1, bp, ML_HEADS, hd, hd),
        s_p[:, :, hd].reshape(1, bp, ML_HEADS, hd),
        m_p.reshape(1, bp, ML_HEADS),
        mqk[t - (CONV_W - 1):, :].reshape(1, bp, CONV_W - 1, 2 * ML_COLS),
        per_head(qkvt_s[c:2 * c], 2 * DA_QK_DIM).reshape(1, bs, ts, DA_HEADS, 2 * DA_QK_DIM),
        per_head(qkvt_s[2 * c:3 * c], DA_V_DIM).reshape(1, bs, ts, DA_HEADS, DA_V_DIM),
        jnp.transpose(c_s, (3, 0, 1, 2)).reshape(1, bs, ML_HEADS, hd, hd),
        jnp.transpose(n_s, (2, 0, 1)).reshape(1, bs, ML_HEADS, hd),
        m_s.reshape(ML_HEADS, bs).T.reshape(1, bs, ML_HEADS),
        jnp.transpose(conv_new_s, (1, 0, 2)).reshape(1, bs, CONV_W - 1, 2 * ML_COLS),
    )
```

```python
import functools
import math

import jax
import jax.numpy as jnp
import numpy as np
from jax import lax
from jax.experimental import pallas as pl
from jax.experimental.pallas import tpu as pltpu

F32 = jnp.float32
BF16 = jnp.bfloat16
I32 = jnp.int32
HI = lax.Precision.HIGHEST

DA_HEADS = 8
DA_QK_DIM = 32
DA_V_DIM = 64
ML_HEADS = 8
ML_HEAD_DIM = 64
CONV_W = 4
NUM_BUCKETS = 32
MAX_DISTANCE = 128
PAGE_SIZE = 128
N_GROUPS = 8
TOPK_GROUPS = 4
TOP_K = 8
ROUTED_SCALE = 2.5
EPS = 1e-6

DA_COLS = DA_HEADS * 2 * DA_QK_DIM
ML_COLS = ML_HEADS * ML_HEAD_DIM
N_GATES = 2 * ML_HEADS
Q_SCALE = DA_QK_DIM ** -0.5
K_SCALE = ML_HEAD_DIM ** -0.5
LOG2E = 1.0 / math.log(2.0)
NEG = -1e30

LANES = 128
SUBLANES = 8
VMEM_LIMIT = 56 * 1024 * 1024

ML_CHUNK = 128
ATT_TILE = 1024
ATT_SUB = 512
ATT_DEN_ROWS = 16
MOE_BM = 256
TOK_BLK = 128
PAGES_PER_STEP = 32
PROJ_TM = 256

NT = (((1,), (1,)), ((), ()))


def _const_spec(shape):
    nd = len(shape)
    return pl.BlockSpec(shape, lambda *_: (0,) * nd)


def _params(sem, vmem=VMEM_LIMIT):
    return pltpu.CompilerParams(dimension_semantics=sem, vmem_limit_bytes=vmem)


def _sigmoid(x):
    return 1.0 / (1.0 + jnp.exp(-x))


def _log_sigmoid(x):
    return jnp.minimum(x, 0.0) - jnp.log1p(jnp.exp(-jnp.abs(x)))


def _lam(lq1, lk1, lq2, lk2, lam_init):
    a = jnp.sum(lq1[...] * lk1[...], axis=-1, keepdims=True)
    b = jnp.sum(lq2[...] * lk2[...], axis=-1, keepdims=True)
    return jnp.exp(a) - jnp.exp(b) + lam_init


def _modulated_norm(x, g, shift, scale):
    y = x * lax.rsqrt(jnp.mean(x * x, axis=-1, keepdims=True) + EPS) * g
    return y * (1.0 + scale) + shift


def _t5_bucket(dist):
    n = jnp.maximum(dist, 0)
    max_exact = NUM_BUCKETS // 2
    nf = jnp.maximum(n, 1).astype(F32)
    large = max_exact + (jnp.log(nf / max_exact) / math.log(MAX_DISTANCE / max_exact)
                         * (NUM_BUCKETS - max_exact)).astype(I32)
    return jnp.where(n < max_exact, n, jnp.minimum(large, NUM_BUCKETS - 1))


def _bias_by_distance(rel_bias, n):
    rb = rel_bias.astype(F32)
    return rb[_t5_bucket(jnp.arange(n))] - rb[NUM_BUCKETS - 1][None, :]


def _adaln_kernel(c_ref, w_ref, b_ref, o_ref):
    c = c_ref[...]
    a = c * _sigmoid(c)
    o_ref[...] = jnp.dot(a, w_ref[...], precision=HI, preferred_element_type=F32) + b_ref[...]


def _adaln(c_all, w, b):
    r, d = c_all.shape
    n = w.shape[1]
    tn = n // 4
    return pl.pallas_call(
        _adaln_kernel,
        grid=(n // tn,),
        in_specs=[_const_spec((r, d)), pl.BlockSpec((d, tn), lambda j: (0, j)),
                  pl.BlockSpec((1, tn), lambda j: (0, j))],
        out_specs=pl.BlockSpec((r, tn), lambda j: (0, j)),
        out_shape=jax.ShapeDtypeStruct((r, n), F32),
        compiler_params=_params(("parallel",)),
        name="adaln",
    )(c_all, w, b.reshape(1, n))


def _inproj_kernel(x_ref, sh_ref, sc_ref, g_ref, wqkv_ref, wml_ref, wg_ref, bg_ref, bgt_ref,
                   qtb_ref, krb_ref, ktf_ref, vtf_ref, vtb_ref, mqk_ref, mv_ref, mo_ref, gates_ref, gatest_ref):
    h = _modulated_norm(x_ref[...], g_ref[...], sh_ref[...], sc_ref[...])
    hb = h.astype(BF16)
    c = DA_COLS
    qkvt = lax.dot_general(wqkv_ref[...], hb, NT, preferred_element_type=F32)
    qtb_ref[...] = (qkvt[0:c, :] * (Q_SCALE * LOG2E)).astype(BF16)
    ktf_ref[...] = qkvt[c:2 * c, :]
    vtf_ref[...] = qkvt[2 * c:3 * c, :]
    vtb_ref[...] = qkvt[2 * c:3 * c, :].astype(BF16)
    krb_ref[...] = lax.dot_general(hb, wqkv_ref[c:2 * c, :], NT, preferred_element_type=F32).astype(BF16)
    ml = lax.dot_general(hb, wml_ref[...], NT, preferred_element_type=F32)
    mqk_ref[...] = ml[:, 0:2 * ML_COLS]
    mv_ref[...] = ml[:, 2 * ML_COLS:3 * ML_COLS].astype(BF16)
    mo_ref[...] = ml[:, 3 * ML_COLS:4 * ML_COLS]
    graw = lax.dot_general(h, wg_ref[...], NT, precision=HI, preferred_element_type=F32) + bg_ref[...]
    lane = lax.broadcasted_iota(I32, graw.shape, 1)
    gates_ref[...] = jnp.where(lane < ML_HEADS, graw, _log_sigmoid(graw))
    grawt = lax.dot_general(wg_ref[...], h, NT, precision=HI, preferred_element_type=F32) + bgt_ref[...]
    sub = lax.broadcasted_iota(I32, grawt.shape, 0)
    gatest_ref[...] = jnp.where(sub < ML_HEADS, grawt, _log_sigmoid(grawt))


def _inproj(x, shift, scale, g, wqkv_t, wml_t, wg_t, b_gate):
    r, d = x.shape
    tm = min(PROJ_TM, r)
    row = lambda n: pl.BlockSpec((tm, n), lambda i: (i, 0))
    col = lambda n: pl.BlockSpec((n, tm), lambda i: (0, i))
    ng = N_GATES
    out_shape = [
        jax.ShapeDtypeStruct((DA_COLS, r), BF16), jax.ShapeDtypeStruct((r, DA_COLS), BF16),
        jax.ShapeDtypeStruct((DA_COLS, r), F32),
        jax.ShapeDtypeStruct((DA_COLS, r), F32), jax.ShapeDtypeStruct((DA_COLS, r), BF16),
        jax.ShapeDtypeStruct((r, 2 * ML_COLS), F32), jax.ShapeDtypeStruct((r, ML_COLS), BF16),
        jax.ShapeDtypeStruct((r, ML_COLS), F32),
        jax.ShapeDtypeStruct((r, ng), F32), jax.ShapeDtypeStruct((ng, r), F32),
    ]
    out_specs = [col(DA_COLS), row(DA_COLS), col(DA_COLS), col(DA_COLS), col(DA_COLS), row(2 * ML_COLS),
                 row(ML_COLS), row(ML_COLS), row(ng), col(ng)]
    return pl.pallas_call(
        _inproj_kernel,
        grid=(r // tm,),
        in_specs=[row(d), _const_spec((1, d)), _const_spec((1, d)), _const_spec((1, d)),
                  _const_spec(wqkv_t.shape), _const_spec(wml_t.shape),
                  _const_spec(wg_t.shape), _const_spec((1, ng)), _const_spec((ng, 1))],
        out_specs=out_specs,
        out_shape=out_shape,
        compiler_params=_params(("parallel",)),
        name="inproj",
    )(x, shift, scale, g, wqkv_t, wml_t, wg_t, b_gate.reshape(1, ng), b_gate.reshape(ng, 1))


def _pair_rmsnorm(x, lane, nd):
    sq = x * x
    lo = jnp.sum(jnp.where(lane < nd, sq, 0.0), axis=-1, keepdims=True) * (1.0 / nd)
    hi = jnp.sum(jnp.where(lane >= nd, sq, 0.0), axis=-1, keepdims=True) * (1.0 / nd)
    return x * jnp.where(lane < nd, lax.rsqrt(lo + EPS), lax.rsqrt(hi + EPS))


def _mlstm_kernel(mqk_ref, mv_ref, og_ref, gates_ref, gatest_ref, cw_ref, cb_ref, mlg_ref,
                  tri_ref, trit_ref, mo_ref, sout_ref, mout_ref, xprev, st, msc):
    c = pl.program_id(0)
    L = mqk_ref.shape[0]
    hd = ML_HEAD_DIM

    @pl.when(c == 0)
    def _():
        xprev[...] = jnp.zeros_like(xprev)
        st[...] = jnp.zeros_like(st)
        msc[...] = jnp.zeros_like(msc)

    x = mqk_ref[...]
    xp = xprev[...]
    row = lax.broadcasted_iota(I32, x.shape, 0)

    def shifted(j):
        return jnp.where(row < j, pltpu.roll(xp, j, 0), pltpu.roll(x, j, 0))

    y = cb_ref[...] + shifted(3) * cw_ref[0:1, :]
    y = y + shifted(2) * cw_ref[1:2, :]
    y = y + shifted(1) * cw_ref[2:3, :]
    y = y + x * cw_ref[3:4, :]
    qk = y * _sigmoid(y)
    xprev[...] = x

    g = gates_ref[...]
    gt = gatest_ref[...]
    fcol_all = jnp.dot(tri_ref[...], g, precision=HI, preferred_element_type=F32)
    frow_all = jnp.dot(gt, trit_ref[...], precision=HI, preferred_element_type=F32)
    r_i = lax.broadcasted_iota(I32, (L, L), 0)
    c_i = lax.broadcasted_iota(I32, (L, L), 1)
    causal = c_i <= r_i
    lane = lax.broadcasted_iota(I32, (L, 2 * hd), 1)

    for p in range(ML_HEADS // 2):
        qpair = qk[:, 2 * hd * p:2 * hd * (p + 1)]
        kpair = qk[:, ML_COLS + 2 * hd * p:ML_COLS + 2 * hd * (p + 1)] * K_SCALE
        vpair = mv_ref[:, 2 * hd * p:2 * hd * (p + 1)].astype(F32)
        hh, decays, vaugs, carries = [], [], [], []
        for j in range(2):
            h = 2 * p + j
            q_h = qpair[:, hd * j:hd * (j + 1)].astype(BF16)
            k_h = kpair[:, hd * j:hd * (j + 1)].astype(BF16)
            s_raw = lax.dot_general(q_h, k_h, NT, preferred_element_type=F32)
            fc = fcol_all[:, ML_HEADS + h:ML_HEADS + h + 1]
            fr = frow_all[ML_HEADS + h:ML_HEADS + h + 1, :]
            igr = gt[h:h + 1, :]
            igc = g[:, h:h + 1]
            m_old = msc[h]
            dlog = jnp.where(causal, fc - fr + igr, NEG)
            inter = m_old + fc
            m_t = jnp.maximum(inter, jnp.max(dlog, axis=-1, keepdims=True))
            dmat = jnp.where(causal, jnp.exp(dlog - m_t), 0.0)
            smat = (s_raw * dmat).astype(BF16)
            vsrc = vpair if j == 0 else pltpu.roll(vpair, hd, 1)
            vaug = jnp.where(lane < hd, vsrc, jnp.where(lane == hd, 1.0, 0.0)).astype(BF16)
            s_h = st[h]
            q_s = jnp.dot(q_h, s_h.astype(BF16), preferred_element_type=F32)
            num = jnp.dot(smat, vaug, preferred_element_type=F32) + jnp.exp(inter - m_t) * q_s
            den = num[:, hd:hd + 1]
            hh.append(num / jnp.maximum(jnp.abs(den), jnp.exp(-m_t)))
            m_new = m_t[L - 1:L, :]
            f_last = fc[L - 1:L, :]
            decays.append(jnp.exp(f_last - fc + igc - m_new))
            carries.append(jnp.exp(m_old + f_last - m_new))
            vaugs.append(vaug)
            msc[h] = m_new
        kd = kpair * jnp.where(lane < hd, decays[0], decays[1])
        kdt = kd.T
        for j in range(2):
            h = 2 * p + j
            upd = jnp.dot(kdt[hd * j:hd * (j + 1), :].astype(BF16), vaugs[j], preferred_element_type=F32)
            st[h] = carries[j] * st[h] + upd
        hpair = jnp.where(lane < hd, hh[0], pltpu.roll(hh[1], hd, 1))
        hn = _pair_rmsnorm(hpair, lane, hd) * mlg_ref[:, 2 * hd * p:2 * hd * (p + 1)]
        og = _sigmoid(og_ref[:, 2 * hd * p:2 * hd * (p + 1)])
        mo_ref[:, 2 * hd * p:2 * hd * (p + 1)] = (hn * og).astype(BF16)

    sout_ref[...] = st[...]
    mout_ref[...] = msc[...]


def _mlstm_prompt(mqk, mv, og, gates, gates_t, conv_w, conv_b, ml_g):
    t = mqk.shape[0]
    L = min(ML_CHUNK, t)
    hd = ML_HEAD_DIM
    tri = jnp.asarray(np.tril(np.ones((L, L), np.float32)))
    row = lambda n: pl.BlockSpec((L, n), lambda i: (i, 0))
    ng = N_GATES
    return pl.pallas_call(
        _mlstm_kernel,
        grid=(t // L,),
        in_specs=[row(2 * ML_COLS), row(ML_COLS), row(ML_COLS), row(ng),
                  pl.BlockSpec((ng, L), lambda i: (0, i)),
                  _const_spec((CONV_W, 2 * ML_COLS)), _const_spec((1, 2 * ML_COLS)),
                  _const_spec((1, ML_COLS)), _const_spec((L, L)), _const_spec((L, L))],
        out_specs=[row(ML_COLS), _const_spec((ML_HEADS, hd, 2 * hd)), _const_spec((ML_HEADS, 1, 1))],
        out_shape=[jax.ShapeDtypeStruct((t, ML_COLS), BF16),
                   jax.ShapeDtypeStruct((ML_HEADS, hd, 2 * hd), F32),
                   jax.ShapeDtypeStruct((ML_HEADS, 1, 1), F32)],
        scratch_shapes=[pltpu.VMEM((L, 2 * ML_COLS), F32), pltpu.VMEM((ML_HEADS, hd, 2 * hd), F32),
                        pltpu.VMEM((ML_HEADS, 1, 1), F32)],
        compiler_params=_params(("arbitrary",)),
        name="mlstm_prompt",
    )(mqk, mv, og, gates, gates_t, conv_w, conv_b.reshape(1, -1), ml_g.reshape(1, -1), tri, tri.T)


def _pattn_kernel(ii_ref, jj_ref, qt_ref, k_ref, vt_ref, bias_ref, dag_ref, lq1, lk1, lq2, lk2,
                  o_ref, m_sc, acc_sc, *, lam_init):
    s_id = pl.program_id(1)
    i = ii_ref[s_id]
    j = jj_ref[s_id]
    dq = DA_QK_DIM
    dv = DA_V_DIM

    @pl.when(j == 0)
    def _():
        m_sc[...] = jnp.full_like(m_sc, NEG)
        acc_sc[...] = jnp.zeros_like(acc_sc)

    def step(use_bias):
        qt = qt_ref[...]
        k = k_ref[...]
        vt = vt_ref[...]
        ones_rows = jnp.ones((ATT_DEN_ROWS, vt.shape[1]), BF16)
        vt_aug = tuple(jnp.concatenate([vt[hp * dv:(hp + 1) * dv, :], ones_rows], axis=0) for hp in range(2))
        tk = k.shape[0]
        sub = min(ATT_SUB, tk)
        work = [(idx, k0) for k0 in range(0, tk, sub) for idx in range(4)]

        def scores(idx, k0):
            c0 = idx * dq
            return jnp.dot(k[k0:k0 + sub, c0:c0 + dq], qt[c0:c0 + dq, :], preferred_element_type=F32)

        st_next = scores(*work[0])
        for n, (idx, k0) in enumerate(work):
            st = st_next
            if n + 1 < len(work):
                st_next = scores(*work[n + 1])
            hp = idx // 2
            if use_bias:
                st = st + bias_ref[hp, k0:k0 + sub, :]
            m_prev = m_sc[idx]
            m_new = jnp.maximum(m_prev, jnp.max(st, axis=0, keepdims=True))
            p = jnp.exp2(st - m_new)
            alpha = jnp.exp2(m_prev - m_new)
            m_sc[idx] = m_new
            acc_sc[idx] = acc_sc[idx] * alpha + jnp.dot(vt_aug[hp][:, k0:k0 + sub], p.astype(BF16),
                                                        preferred_element_type=F32)

    near = j >= i - 1
    pl.when(near)(lambda: step(True))
    pl.when(jnp.logical_not(near))(lambda: step(False))

    @pl.when(j == i)
    def _():
        lam = _lam(lq1, lk1, lq2, lk2, lam_init)
        for hp in range(2):
            a1 = acc_sc[2 * hp]
            a2 = acc_sc[2 * hp + 1]
            o = a1[0:dv, :] / a1[dv:dv + 1, :] - lam * (a2[0:dv, :] / a2[dv:dv + 1, :])
            o = o * lax.rsqrt(jnp.mean(o * o, axis=0, keepdims=True) + EPS)
            o_ref[hp * dv:(hp + 1) * dv, :] = o * dag_ref[hp * dv:(hp + 1) * dv, :] * (1.0 - lam_init)


def _prompt_bias_tiles(rel_bias, tile):
    rb = rel_bias.astype(F32)
    rb = (rb - rb[NUM_BUCKETS - 1][None, :]) * LOG2E
    kpos = jnp.arange(tile)[:, None]
    qpos = jnp.arange(tile)[None, :]
    dist = jnp.stack([tile + qpos - kpos, qpos - kpos], axis=0)
    bucket = _t5_bucket(dist)[None]
    tiles = jnp.zeros((DA_HEADS, 2, tile, tile), F32)
    for b in range(NUM_BUCKETS - 1):
        tiles = jnp.where(bucket == b, rb[b][:, None, None, None], tiles)
    tiles = jnp.where((dist >= 0)[None], tiles, NEG)
    return tiles.reshape(DA_HEADS // 2, 2, 2, tile, tile)


def _prompt_attention(qt_bf, k_bf, vt_bf, rel_bias, da_g, lams, lam_init):
    t = k_bf.shape[0]
    tile = min(ATT_TILE, t)
    assert tile > MAX_DISTANCE or tile == t
    nq = t // tile
    ii, jj = [], []
    for i in range(nq):
        for j in range(i + 1):
            ii.append(i)
            jj.append(j)
    ii = jnp.asarray(np.array(ii, np.int32))
    jj = jnp.asarray(np.array(jj, np.int32))
    bias = _prompt_bias_tiles(rel_bias, tile)
    w = 2 * DA_V_DIM
    lam_specs = [_const_spec((1, DA_QK_DIM))] * 4
    grid_spec = pltpu.PrefetchScalarGridSpec(
        num_scalar_prefetch=2,
        grid=(DA_HEADS // 2, ii.shape[0]),
        in_specs=[
            pl.BlockSpec((w, tile), lambda p, s, ii, jj: (p, ii[s])),
            pl.BlockSpec((tile, w), lambda p, s, ii, jj: (jj[s], p)),
            pl.BlockSpec((w, tile), lambda p, s, ii, jj: (p, jj[s])),
            pl.BlockSpec((None, 2, None, tile, tile),
                         lambda p, s, ii, jj: (p, 0, jnp.clip(jj[s] - ii[s] + 1, 0, 1), 0, 0)),
            pl.BlockSpec((w, 1), lambda p, s, ii, jj: (p, 0)),
        ] + lam_specs,
        out_specs=pl.BlockSpec((w, tile), lambda p, s, ii, jj: (p, ii[s])),
        scratch_shapes=[pltpu.VMEM((4, 1, tile), F32), pltpu.VMEM((4, DA_V_DIM + ATT_DEN_ROWS, tile), F32)],
    )
    return pl.pallas_call(
        functools.partial(_pattn_kernel, lam_init=lam_init),
        grid_spec=grid_spec,
        out_shape=jax.ShapeDtypeStruct((DA_COLS, t), F32),
        compiler_params=_params(("parallel", "arbitrary")),
        name="prompt_attention",
    )(ii, jj, qt_bf, k_bf, vt_bf, bias, da_g.reshape(-1, 1), *lams)


def _sinproj_kernel(x_ref, sh_ref, sc_ref, g_ref, wqkv_ref, wml_ref, wg_ref, bgt_ref,
                    qkvt_ref, mqk_ref, mvot_ref, gatest_ref):
    h = _modulated_norm(x_ref[...], g_ref[...], sh_ref[...], sc_ref[...])
    hb = h.astype(BF16)
    qkvt_ref[...] = lax.dot_general(wqkv_ref[...], hb, NT, preferred_element_type=F32)
    mqk_ref[...] = lax.dot_general(hb, wml_ref[0:2 * ML_COLS, :], NT, preferred_element_type=F32)
    mvot_ref[...] = lax.dot_general(wml_ref[2 * ML_COLS:4 * ML_COLS, :], hb, NT, preferred_element_type=F32)
    grawt = lax.dot_general(wg_ref[...], h, NT, precision=HI, preferred_element_type=F32) + bgt_ref[...]
    sub = lax.broadcasted_iota(I32, grawt.shape, 0)
    gatest_ref[...] = jnp.where(sub < ML_HEADS, grawt, _log_sigmoid(grawt))


def _sample_inproj(x, shift, scale, g, wqkv_t, wml_t, wg_t, b_gate):
    b, d = x.shape
    return pl.pallas_call(
        _sinproj_kernel,
        out_shape=[jax.ShapeDtypeStruct((3 * DA_COLS, b), F32), jax.ShapeDtypeStruct((b, 2 * ML_COLS), F32),
                   jax.ShapeDtypeStruct((2 * ML_COLS, b), F32), jax.ShapeDtypeStruct((N_GATES, b), F32)],
        compiler_params=pltpu.CompilerParams(vmem_limit_bytes=VMEM_LIMIT),
        name="sample_inproj",
    )(x, shift, scale, g, wqkv_t, wml_t, wg_t, b_gate.reshape(N_GATES, 1))


def _sconv_kernel(mqk_ref, buf_ref, cw_ref, cb_ref, qkt_ref, new_ref):
    x = mqk_ref[...]
    y = cb_ref[...] + buf_ref[0] * cw_ref[0:1, :]
    y = y + buf_ref[1] * cw_ref[1:2, :]
    y = y + buf_ref[2] * cw_ref[2:3, :]
    y = y + x * cw_ref[3:4, :]
    a = y * _sigmoid(y)
    lane = lax.broadcasted_iota(I32, a.shape, 1)
    qkt_ref[...] = jnp.where(lane < ML_COLS, a, a * K_SCALE).T
    new_ref[0] = buf_ref[1]
    new_ref[1] = buf_ref[2]
    new_ref[2] = x


def _sample_conv(mqk, conv_buf, conv_w, conv_b):
    b, n = mqk.shape
    return pl.pallas_call(
        _sconv_kernel,
        out_shape=[jax.ShapeDtypeStruct((n, b), F32), jax.ShapeDtypeStruct(conv_buf.shape, F32)],
        compiler_params=pltpu.CompilerParams(vmem_limit_bytes=VMEM_LIMIT),
        name="sample_conv",
    )(mqk, conv_buf, conv_w, conv_b.reshape(1, -1))


def _smlstm_kernel(q_ref, k_ref, v_ref, og_ref, ig_ref, lf_ref, m_ref, c_ref, n_ref, mlg_ref,
                   co_ref, no_ref, mo_ref, ho_ref):
    hd = ML_HEAD_DIM
    ig = ig_ref[...]
    lf = lf_ref[...]
    m_old = m_ref[...]
    m_t = jnp.maximum(m_old + lf, ig)
    a = jnp.exp(ig - m_t)
    w = jnp.exp(m_old + lf - m_t)
    v = v_ref[...]

    def body(d, num):
        cn = w * c_ref[d] + (a * k_ref[pl.ds(d, 1), :]) * v
        co_ref[d] = cn
        return num + q_ref[pl.ds(d, 1), :] * cn

    num = lax.fori_loop(0, hd, body, jnp.zeros(v.shape, F32))
    nn = w * n_ref[...] + a * k_ref[...]
    den = jnp.sum(q_ref[...] * nn, axis=0, keepdims=True)
    hv = num / jnp.maximum(jnp.abs(den), jnp.exp(-m_t))
    hn = hv * lax.rsqrt(jnp.mean(hv * hv, axis=0, keepdims=True) + EPS) * mlg_ref[...]
    no_ref[...] = nn
    mo_ref[...] = m_t
    ho_ref[...] = hn * _sigmoid(og_ref[...])


def _sample_mlstm(qk_t, v_t, og_t, gates_t, m_t, c_t, n_t, ml_g):
    b = qk_t.shape[1]
    h, d = ML_HEADS, ML_HEAD_DIM
    head_rows = lambda off: pl.BlockSpec((d, b), lambda i: (i + off, 0))
    vec = pl.BlockSpec((None, 1, b), lambda i: (i, 0, 0))
    return pl.pallas_call(
        _smlstm_kernel,
        grid=(h,),
        in_specs=[head_rows(0), head_rows(h), head_rows(0), head_rows(0),
                  vec, pl.BlockSpec((None, 1, b), lambda i: (i + h, 0, 0)), vec,
                  pl.BlockSpec((None, d, d, b), lambda i: (i, 0, 0, 0)),
                  pl.BlockSpec((None, d, b), lambda i: (i, 0, 0)),
                  pl.BlockSpec((None, d, 1), lambda i: (i, 0, 0))],
        out_specs=[pl.BlockSpec((None, d, d, b), lambda i: (i, 0, 0, 0)),
                   pl.BlockSpec((None, d, b), lambda i: (i, 0, 0)), vec, head_rows(0)],
        out_shape=[jax.ShapeDtypeStruct((h, d, d, b), F32), jax.ShapeDtypeStruct((h, d, b), F32),
                   jax.ShapeDtypeStruct((h, 1, b), F32), jax.ShapeDtypeStruct((h * d, b), F32)],
        compiler_params=_params(("parallel",)),
        name="sample_mlstm",
    )(qk_t, qk_t, v_t, og_t, gates_t.reshape(2 * h, 1, b), gates_t.reshape(2 * h, 1, b),
      m_t.reshape(h, 1, b), c_t, n_t, ml_g.reshape(h, d, 1))


def _sattn_kernel(pt_ref, qt_ref, knt_ref, vnt_ref, bpage_ref, bnew_ref, dag_ref, lq1, lk1, lq2, lk2,
                  ck_ref, cv_ref, o_ref, s_sc, a_sc, acc, qb, kbuf, vbuf, ksem, vsem, *, n_pg, lam_init):
    b = pl.program_id(0)
    ph = pl.program_id(1)
    g = pl.program_id(2)
    n_b = pl.num_programs(0)
    n_g = pl.num_programs(2)
    dq = DA_QK_DIM
    dv = DA_V_DIM
    nh = DA_HEADS
    n_pages = s_sc.shape[0] - 1
    lane = lax.broadcasted_iota(I32, qt_ref.shape, 1)

    def group_copies(cache_ref, buf, sem, bq, gq, slot):
        return [pltpu.make_async_copy(cache_ref.at[pt_ref[gq * n_pg + i, bq]], buf.at[slot, i], sem.at[slot])
                for i in range(n_pg)]

    def start_group(phq, bq, gq):
        slot = (bq * n_g + gq) % 2

        @pl.when(phq == 0)
        def _():
            for cp in group_copies(ck_ref, kbuf, ksem, bq, gq, slot):
                cp.start()

        @pl.when(phq == 1)
        def _():
            for cp in group_copies(cv_ref, vbuf, vsem, bq, gq, slot):
                cp.start()

    step = (b * 2 + ph) * n_g + g
    slot = (b * n_g + g) % 2

    @pl.when(step == 0)
    def _():
        start_group(ph, b, g)

    @pl.when(step + 1 < n_b * 2 * n_g)
    def _():
        nxt = step + 1
        start_group((nxt // n_g) % 2, nxt // (2 * n_g), nxt % n_g)

    k_refs = [kbuf.at[slot, i] for i in range(n_pg)]
    v_refs = [vbuf.at[slot, i] for i in range(n_pg)]

    def column(ref):
        return jnp.sum(jnp.where(lane == b, ref[...], 0.0), axis=1, keepdims=True)

    @pl.when((b == 0) & (ph == 0) & (g == 0))
    def _():
        o_ref[...] = jnp.zeros_like(o_ref)

    @pl.when((ph == 0) & (g == 0))
    def _():
        qb[...] = jnp.broadcast_to(column(qt_ref) * Q_SCALE, qb.shape)

    @pl.when(ph == 0)
    def _():
        for cp in group_copies(ck_ref, kbuf, ksem, b, g, slot):
            cp.wait()
        for c in range(2 * nh):
            qc = qb[c * dq:(c + 1) * dq, :]
            srow = (c % 2) * nh + c // 2
            for i in range(n_pg):
                prod = k_refs[i][c * dq:(c + 1) * dq, :] * qc
                s_sc[g * n_pg + i, srow:srow + 1, :] = jnp.sum(prod, axis=0, keepdims=True)

        @pl.when(g == n_g - 1)
        def _():
            s_sc[n_pages - 1] = s_sc[n_pages - 1] + bpage_ref[...]
            prodn = column(knt_ref) * qb[:, 0:1]
            sub = lax.broadcasted_iota(I32, (2 * nh, 1), 0)
            ln = bnew_ref[...]
            for r in range(2 * nh):
                c = (r % nh) * 2 + r // nh
                ln = ln + jnp.where(sub == r, jnp.sum(prodn[c * dq:(c + 1) * dq, :], axis=0, keepdims=True), 0.0)
            lane_s = lax.broadcasted_iota(I32, (2 * nh, LANES), 1)
            s_sc[n_pages] = jnp.where(lane_s == 0, ln, NEG)
            s_all = s_sc[...]
            mx = jnp.max(jnp.max(s_all, axis=0), axis=1, keepdims=True)
            p = jnp.exp(s_all - mx)
            l = jnp.sum(jnp.sum(p, axis=0), axis=1, keepdims=True)
            lam = _lam(lq1, lk1, lq2, lk2, lam_init)
            a_sc[...] = p[:, 0:nh, :] / l[0:nh, :] - lam * (p[:, nh:2 * nh, :] / l[nh:2 * nh, :])
            acc[...] = jnp.zeros_like(acc)

    @pl.when(ph == 1)
    def _():
        for cp in group_copies(cv_ref, vbuf, vsem, b, g, slot):
            cp.wait()
        for h in range(nh):
            part = acc[h * dv:(h + 1) * dv, :]
            for i in range(n_pg):
                part = part + v_refs[i][h * dv:(h + 1) * dv, :] * a_sc[g * n_pg + i, h:h + 1, :]
            acc[h * dv:(h + 1) * dv, :] = part

        @pl.when(g == n_g - 1)
        def _():
            out = jnp.sum(acc[...], axis=1, keepdims=True)
            vn = column(vnt_ref)
            a_new = a_sc[n_pages][:, 0:1]
            pieces = []
            for h in range(nh):
                oh = out[h * dv:(h + 1) * dv, :] + a_new[h:h + 1, :] * vn[h * dv:(h + 1) * dv, :]
                pieces.append(oh * lax.rsqrt(jnp.mean(oh * oh, axis=0, keepdims=True) + EPS))
            res = jnp.concatenate(pieces, axis=0) * dag_ref[...] * (1.0 - lam_init)
            o_ref[...] = jnp.where(lane == b, res, o_ref[...])


def _sample_attention(q_t, kn_t, vn_t, ck_t, cv_t, page_table_t, rel_bias, da_g, lams, lam_init):
    w, b = q_t.shape
    n_pages = page_table_t.shape[0]
    n_pg = min(PAGES_PER_STEP, n_pages)
    n_g = n_pages // n_pg
    ps = PAGE_SIZE
    past = n_pages * ps
    nh = DA_HEADS
    tab = _bias_by_distance(rel_bias, ps + 1)
    bpage = jnp.tile(tab[ps - jnp.arange(ps)].T, (2, 1))
    bnew = jnp.tile(tab[0:1].T, (2, 1))

    in_specs = [_const_spec((w, b))] * 3 + [_const_spec((2 * nh, ps)), _const_spec((2 * nh, 1)),
                                            _const_spec((w, 1))] + [_const_spec((1, DA_QK_DIM))] * 4
    in_specs += [pl.BlockSpec(memory_space=pl.ANY)] * 2
    grid_spec = pltpu.PrefetchScalarGridSpec(
        num_scalar_prefetch=1,
        grid=(b, 2, n_g),
        in_specs=in_specs,
        out_specs=_const_spec((w, b)),
        scratch_shapes=[pltpu.VMEM((n_pages + 1, 2 * nh, ps), F32), pltpu.VMEM((n_pages + 1, nh, ps), F32),
                        pltpu.VMEM((w, ps), F32), pltpu.VMEM((w, ps), F32),
                        pltpu.VMEM((2, n_pg, w, ps), F32), pltpu.VMEM((2, n_pg, w, ps), F32),
                        pltpu.SemaphoreType.DMA((2,)), pltpu.SemaphoreType.DMA((2,))],
    )
    return pl.pallas_call(
        functools.partial(_sattn_kernel, n_pg=n_pg, lam_init=lam_init),
        grid_spec=grid_spec,
        out_shape=jax.ShapeDtypeStruct((w, b), F32),
        compiler_params=_params(("arbitrary", "arbitrary", "arbitrary")),
        name="sample_attention",
    )(page_table_t, q_t, kn_t, vn_t, bpage, bnew, da_g.reshape(w, 1), *lams, ck_t, cv_t)


def _outproj_kernel(ap_ref, mop_ref, xp_ref, g1p_ref, sh2p_ref, sc2p_ref, g2p_ref,
                    as_ref, mos_ref, xs_ref, g1s_ref, sh2s_ref, sc2s_ref, g2s_ref,
                    wo_ref, n2_ref, wrt_ref, wsg_ref, wsu_ref, wsd_ref,
                    x1p_ref, x1s_ref, h2t_ref, lt_ref):
    i = pl.program_id(0)
    n_prompt = pl.num_programs(0) - 1

    def block(a, mo, x, g1, sh2, sc2, g2, out_ref):
        half = a.shape[1]
        mix = jnp.dot(a, wo_ref[0:half, :], preferred_element_type=F32)
        mix = mix + jnp.dot(mo, wo_ref[half:2 * half, :], preferred_element_type=F32)
        x1 = x + g1 * mix
        h2 = _modulated_norm(x1, n2_ref[...], sh2, sc2)
        lt_ref[...] = lax.dot_general(wrt_ref[...], h2, NT, precision=HI, preferred_element_type=F32)
        tm = h2.shape[0]
        for s in range(h2.shape[1] // LANES):
            h2t_ref[pl.ds(s, tm, stride=SUBLANES), :] = h2[:, s * LANES:(s + 1) * LANES]
        hb = h2.astype(BF16)
        gate = jnp.dot(hb, wsg_ref[...], preferred_element_type=F32)
        up = jnp.dot(hb, wsu_ref[...], preferred_element_type=F32)
        act = (gate * _sigmoid(gate) * up).astype(BF16)
        shared = jnp.dot(act, wsd_ref[...], preferred_element_type=F32)
        out_ref[...] = x1 + g2 * shared

    @pl.when(i < n_prompt)
    def _():
        block(ap_ref[...].T.astype(BF16), mop_ref[...], xp_ref[...], g1p_ref[...], sh2p_ref[...], sc2p_ref[...],
              g2p_ref[...], x1p_ref)

    @pl.when(i == n_prompt)
    def _():
        block(as_ref[...].T.astype(BF16), mos_ref[...].T.astype(BF16), xs_ref[...], g1s_ref[...], sh2s_ref[...],
              sc2s_ref[...], g2s_ref[...], x1s_ref)


def _outproj(a_p, mo_p, x_p, mod_p, a_ts, mo_ts, x_s, mod_s, w_out, n2, w_router_t, wsg, wsu, wsd):
    t, d = x_p.shape
    bs = x_s.shape[0]
    tm = TOK_BLK
    assert bs == tm and t % tm == 0
    nbp = t // tm
    e = w_router_t.shape[0]
    nsl = d // LANES
    prow = lambda n: pl.BlockSpec((tm, n), lambda i: (jnp.minimum(i, nbp - 1), 0))
    pcol = pl.BlockSpec((DA_COLS, tm), lambda i: (0, jnp.minimum(i, nbp - 1)))
    in_specs = ([pcol, prow(ML_COLS), prow(d)] + [_const_spec((1, d))] * 4
                + [_const_spec((DA_COLS, bs)), _const_spec((ML_COLS, bs)), _const_spec((bs, d))]
                + [_const_spec((bs, d))] * 4
                + [_const_spec(w_out.shape), _const_spec((1, d)), _const_spec(w_router_t.shape),
                   _const_spec(wsg.shape), _const_spec(wsu.shape), _const_spec(wsd.shape)])
    return pl.pallas_call(
        _outproj_kernel,
        grid=(nbp + 1,),
        in_specs=in_specs,
        out_specs=[prow(d), _const_spec((bs, d)), pl.BlockSpec((tm * nsl, LANES), lambda i: (i, 0)),
                   pl.BlockSpec((e, tm), lambda i: (0, i))],
        out_shape=[jax.ShapeDtypeStruct((t, d), F32), jax.ShapeDtypeStruct((bs, d), F32),
                   jax.ShapeDtypeStruct(((t + bs) * nsl, LANES), F32), jax.ShapeDtypeStruct((e, t + bs), F32)],
        compiler_params=_params(("arbitrary",)),
        name="outproj",
    )(a_p, mo_p, x_p, *mod_p, a_ts, mo_ts, x_s, *mod_s, w_out, n2, w_router_t, wsg, wsu, wsd)


def _first_max(cur, iota, n, axis=0):
    t = jnp.max(cur, axis=axis, keepdims=True)
    idx = jnp.min(jnp.where(cur == t, iota, float(n)), axis=axis, keepdims=True)
    return t, idx


def _route_kernel(lt_ref, bias_ref, ut_ref, ones_ref, eidx_ref, w_ref, rank_ref, cnt_ref, carry):
    i = pl.program_id(0)

    @pl.when(i == 0)
    def _():
        carry[...] = jnp.zeros_like(carry)

    aff = _sigmoid(lt_ref[...])
    sel = aff + bias_ref[...]
    n_e, n_t = aff.shape
    gsz = n_e // N_GROUPS
    e_iota = lax.broadcasted_iota(I32, (n_e, n_t), 0).astype(F32)
    g_iota = lax.broadcasted_iota(I32, (N_GROUPS, n_t), 0).astype(F32)
    in_iota = lax.broadcasted_iota(I32, (gsz, n_t), 0).astype(F32)
    ninf = -jnp.inf
    gscore = jnp.zeros((N_GROUPS, n_t), F32)
    for g in range(N_GROUPS):
        blk = sel[g * gsz:(g + 1) * gsz, :]
        t1, i1 = _first_max(blk, in_iota, gsz)
        t2 = jnp.max(jnp.where(in_iota == i1, ninf, blk), axis=0, keepdims=True)
        gscore = jnp.where(g_iota == float(g), t1 + t2, gscore)
    gmask = jnp.zeros((N_GROUPS, n_t), F32)
    cur = gscore
    for _ in range(TOPK_GROUPS):
        _, gi = _first_max(cur, g_iota, N_GROUPS)
        hit = g_iota == gi
        gmask = jnp.where(hit, 1.0, gmask)
        cur = jnp.where(hit, ninf, cur)
    cur = jnp.concatenate(
        [jnp.where(gmask[g:g + 1, :] > 0.0, sel[g * gsz:(g + 1) * gsz, :], ninf) for g in range(N_GROUPS)], axis=0)
    hits, ws = [], []
    msum = jnp.zeros((n_e, n_t), F32)
    for k in range(TOP_K):
        _, ei = _first_max(cur, e_iota, n_e)
        hit = e_iota == ei
        ws.append(jnp.sum(jnp.where(hit, aff, 0.0), axis=0, keepdims=True))
        cur = jnp.where(hit, ninf, cur)
        msum = msum + hit.astype(F32)
        hits.append(hit)
        eidx_ref[k:k + 1, :] = ei.astype(I32)
    wsum = ws[0]
    for k in range(1, TOP_K):
        wsum = wsum + ws[k]
    mb = msum.astype(BF16)
    base = carry[...] + jnp.dot(mb, ut_ref[...], preferred_element_type=F32)
    for k in range(TOP_K):
        w_ref[k:k + 1, :] = ws[k] / wsum * ROUTED_SCALE
        rank_ref[k:k + 1, :] = jnp.sum(jnp.where(hits[k], base, 0.0), axis=0, keepdims=True).astype(I32)
    carry[...] = carry[...] + jnp.dot(mb, ones_ref[...], preferred_element_type=F32)
    cnt_ref[...] = carry[...]


def _route(logits_t, router_bias):
    n_e, t_all = logits_t.shape
    tb = TOK_BLK
    ut = jnp.asarray(np.triu(np.ones((tb, tb), np.float32), 1), dtype=BF16)
    ones = jnp.ones((tb, tb), BF16)
    bias_b = jnp.broadcast_to(router_bias.astype(F32)[:, None], (n_e, tb))
    tok = pl.BlockSpec((TOP_K, tb), lambda i: (0, i))
    return pl.pallas_call(
        _route_kernel,
        grid=(t_all // tb,),
        in_specs=[pl.BlockSpec((n_e, tb), lambda i: (0, i)), _const_spec((n_e, tb)), _const_spec((tb, tb)),
                  _const_spec((tb, tb))],
        out_specs=[tok, tok, tok, _const_spec((n_e, tb))],
        out_shape=[jax.ShapeDtypeStruct((TOP_K, t_all), I32), jax.ShapeDtypeStruct((TOP_K, t_all), F32),
                   jax.ShapeDtypeStruct((TOP_K, t_all), I32), jax.ShapeDtypeStruct((n_e, tb), F32)],
        scratch_shapes=[pltpu.VMEM((n_e, tb), F32)],
        compiler_params=_params(("arbitrary",)),
        name="moe_route",
    )(logits_t, bias_b, ut, ones)


def _pos_kernel(eidx_ref, rank_ref, offs_ref, pos_ref):
    offs = offs_ref[...]
    e_iota = lax.broadcasted_iota(I32, offs.shape, 0)
    for k in range(TOP_K):
        hit = e_iota == eidx_ref[k:k + 1, :]
        base = jnp.sum(jnp.where(hit, offs, 0.0), axis=0, keepdims=True).astype(I32)
        pos_ref[k:k + 1, :] = base + rank_ref[k:k + 1, :]


def _positions(eidx, rank, offs):
    t_all = eidx.shape[1]
    tb = TOK_BLK
    n_e = offs.shape[0]
    tok = pl.BlockSpec((TOP_K, tb), lambda i: (0, i))
    return pl.pallas_call(
        _pos_kernel,
        grid=(t_all // tb,),
        in_specs=[tok, tok, _const_spec((n_e, tb))],
        out_specs=tok,
        out_shape=jax.ShapeDtypeStruct((TOP_K, t_all), I32),
        compiler_params=_params(("parallel",)),
        name="moe_positions",
    )(eidx, rank, jnp.broadcast_to(offs.astype(F32)[:, None], (n_e, tb)))


def _tile_start(row):
    start = row * SUBLANES
    return start if isinstance(start, int) else pl.multiple_of(start, SUBLANES)


def _row_copy(src, src_row, dst, dst_row, sem):
    return pltpu.make_async_copy(src.at[pl.ds(_tile_start(src_row), SUBLANES)],
                                 dst.at[pl.ds(_tile_start(dst_row), SUBLANES)], sem)


def _dispatch_kernel(pos_ref, h2t_ref, xs_in_ref, xs_ref, sem):
    del xs_in_ref
    tb = pos_ref.shape[1]

    def issue(t, carry):
        for k in range(TOP_K):
            _row_copy(h2t_ref, t, xs_ref, pos_ref[k, t], sem).start(priority=k % 2)
        return carry

    lax.fori_loop(0, tb, issue, 0)

    def drain(t, carry):
        for k in range(TOP_K):
            _row_copy(h2t_ref, 0, xs_ref, 0, sem).wait()
        return carry

    lax.fori_loop(0, tb, drain, 0)


def _dispatch(pos, h2t, n_rows):
    t_all = pos.shape[1]
    tb = TOK_BLK
    xs0 = jnp.zeros((n_rows * SUBLANES, LANES), F32)
    return pl.pallas_call(
        _dispatch_kernel,
        grid=(t_all // tb,),
        in_specs=[pl.BlockSpec((TOP_K, tb), lambda i: (0, i), memory_space=pltpu.SMEM),
                  pl.BlockSpec((tb * SUBLANES, LANES), lambda i: (i, 0)), pl.BlockSpec(memory_space=pl.ANY)],
        out_specs=pl.BlockSpec(memory_space=pl.ANY),
        out_shape=jax.ShapeDtypeStruct(xs0.shape, F32),
        input_output_aliases={2: 0},
        scratch_shapes=[pltpu.SemaphoreType.DMA(())],
        compiler_params=_params(("arbitrary",)),
        name="moe_dispatch",
    )(pos, h2t, xs0)


def _experts_kernel(be_ref, nv_ref, xs_ref, wg_ref, wu_ref, wd_ref, ys_ref, wgb, wub, wdb):
    i = pl.program_id(0)

    @pl.when(i < nv_ref[0])
    def _():
        @pl.when((i == 0) | (be_ref[i] != be_ref[jnp.maximum(i - 1, 0)]))
        def _():
            wgb[...] = wg_ref[...].astype(BF16)
            wub[...] = wu_ref[...].astype(BF16)
            wdb[...] = wd_ref[...].astype(BF16)

        bm = xs_ref.shape[0] // SUBLANES
        nsl = wg_ref.shape[0] // LANES
        x = jnp.concatenate([xs_ref[pl.ds(s, bm, stride=SUBLANES), :] for s in range(nsl)], axis=1).astype(BF16)
        gate = jnp.dot(x, wgb[...], preferred_element_type=F32)
        up = jnp.dot(x, wub[...], preferred_element_type=F32)
        act = (gate * _sigmoid(gate) * up).astype(BF16)
        y = jnp.dot(act, wdb[...], preferred_element_type=F32)
        for s in range(nsl):
            ys_ref[pl.ds(s, bm, stride=SUBLANES), :] = y[:, s * LANES:(s + 1) * LANES]


def _experts(block_e, n_valid, xs, w_g, w_u, w_d, n_blocks):
    bm = MOE_BM
    d, de = w_g.shape[1], w_g.shape[2]
    rows = pl.BlockSpec((bm * SUBLANES, LANES), lambda i, be, nv: (jnp.minimum(i, nv[0] - 1), 0))
    grid_spec = pltpu.PrefetchScalarGridSpec(
        num_scalar_prefetch=2,
        grid=(n_blocks,),
        in_specs=[rows,
                  pl.BlockSpec((None, d, de), lambda i, be, nv: (be[i], 0, 0)),
                  pl.BlockSpec((None, d, de), lambda i, be, nv: (be[i], 0, 0)),
                  pl.BlockSpec((None, de, d), lambda i, be, nv: (be[i], 0, 0))],
        out_specs=rows,
        scratch_shapes=[pltpu.VMEM((d, de), BF16), pltpu.VMEM((d, de), BF16), pltpu.VMEM((de, d), BF16)],
    )
    return pl.pallas_call(
        _experts_kernel,
        grid_spec=grid_spec,
        out_shape=jax.ShapeDtypeStruct(xs.shape, F32),
        input_output_aliases={2: 0},
        compiler_params=_params(("arbitrary",)),
        name="moe_experts",
    )(block_e, n_valid, xs, w_g, w_u, w_d)


def _combine_kernel(pos_ref, w_ref, x1p_ref, g2p_ref, x1s_ref, g2s_ref, fg_ref, ys_ref, yp_ref, ysm_ref, buf, sem):
    i = pl.program_id(0)
    n_prompt = pl.num_programs(0) - 1
    tb = pos_ref.shape[1]
    nsl = x1p_ref.shape[1] // LANES

    def issue(t, carry):
        for k in range(TOP_K):
            pltpu.make_async_copy(ys_ref.at[pl.ds(_tile_start(pos_ref[k, t]), SUBLANES)],
                                  buf.at[k, pl.ds(_tile_start(t), SUBLANES)], sem).start(priority=k % 2)
        return carry

    lax.fori_loop(0, tb, issue, 0)

    def drain(t, carry):
        for k in range(TOP_K):
            pltpu.make_async_copy(ys_ref.at[pl.ds(0, SUBLANES)], buf.at[k, pl.ds(0, SUBLANES)], sem).wait()
        return carry

    lax.fori_loop(0, tb, drain, 0)

    w = w_ref[...]
    routed = None
    for k in range(TOP_K):
        yk = jnp.concatenate([buf[k, pl.ds(s, tb, stride=SUBLANES), :] for s in range(nsl)], axis=1)
        term = yk * w[:, k:k + 1]
        routed = term if routed is None else routed + term

    def finish(x1, g2, out_ref):
        x2 = x1 + g2 * routed
        out_ref[...] = x2 * lax.rsqrt(jnp.mean(x2 * x2, axis=-1, keepdims=True) + EPS) * fg_ref[...]

    @pl.when(i < n_prompt)
    def _():
        finish(x1p_ref[...], g2p_ref[...], yp_ref)

    @pl.when(i == n_prompt)
    def _():
        finish(x1s_ref[...], g2s_ref[...], ysm_ref)


def _combine(pos, w_tm, x1_p, g2_p, x1_s, g2_s, final_g, ys):
    t, d = x1_p.shape
    bs = x1_s.shape[0]
    tb = TOK_BLK
    assert bs == tb and t % tb == 0
    nbp = t // tb
    prow = pl.BlockSpec((tb, d), lambda i: (jnp.minimum(i, nbp - 1), 0))
    return pl.pallas_call(
        _combine_kernel,
        grid=(nbp + 1,),
        in_specs=[pl.BlockSpec((TOP_K, tb), lambda i: (0, i), memory_space=pltpu.SMEM),
                  pl.BlockSpec((tb, TOP_K), lambda i: (i, 0)),
                  prow, _const_spec((1, d)), _const_spec((bs, d)), _const_spec((bs, d)), _const_spec((1, d)),
                  pl.BlockSpec(memory_space=pl.ANY)],
        out_specs=[prow, _const_spec((bs, d))],
        out_shape=[jax.ShapeDtypeStruct((t, d), F32), jax.ShapeDtypeStruct((bs, d), F32)],
        scratch_shapes=[pltpu.VMEM((TOP_K, tb * SUBLANES, LANES), F32), pltpu.SemaphoreType.DMA(())],
        compiler_params=_params(("arbitrary",)),
        name="moe_combine",
    )(pos, w_tm, x1_p, g2_p, x1_s, g2_s, final_g.reshape(1, d), ys)


def kernel(x_prompt, x_sample, cache_k, cache_v, state_C, state_n, state_m, state_conv, page_table, c_prompt, c_sample, w_ada, b_ada, norm1_g, norm2_g, w_in, conv_w, conv_b, lambda_q1, lambda_k1, lambda_q2, lambda_k2, da_norm_g, rel_bias, ml_b_i, ml_b_f, ml_norm_g, w_out, w_router, router_bias, w_exp_gate, w_exp_up, w_exp_down, w_sh_gate, w_sh_up, w_sh_down, final_norm_g):
    bp, t, d = x_prompt.shape
    bs, ts, _ = x_sample.shape
    depth = w_ada.shape[0]
    assert bp == 1 and ts == 1 and depth == 1
    assert d % LANES == 0 and (t + bs) % TOK_BLK == 0 and t % TOK_BLK == 0
    l = 0
    lam_init = 0.8 - 0.6 * math.exp(-0.3 * l)
    lams = [a[l].reshape(1, -1).astype(F32) for a in (lambda_q1, lambda_k1, lambda_q2, lambda_k2)]
    t_all = t + bs
    hd = ML_HEAD_DIM

    pad = (-(bp + bs)) % SUBLANES
    c_all = jnp.concatenate([c_prompt, c_sample, jnp.zeros((pad, d), F32)], axis=0)
    mod = _adaln(c_all, w_ada[l], b_ada[l])
    modp = [mod[0:1, i * d:(i + 1) * d] for i in range(6)]
    mods = [mod[1:1 + bs, i * d:(i + 1) * d] for i in range(6)]

    w_in_t = w_in[l].T
    c = DA_COLS
    wqkv_t = w_in_t[0:3 * c].astype(BF16)
    wml_t = w_in_t[3 * c:3 * c + 4 * ML_COLS].astype(BF16)
    wg_t = w_in_t[3 * c + 4 * ML_COLS:]
    b_gate = jnp.concatenate([ml_b_i[l], ml_b_f[l]]).astype(F32)
    w_out_b = w_out[l].astype(BF16)
    w_router_t = w_router[l].T
    wsg, wsu, wsd = w_sh_gate[l].astype(BF16), w_sh_up[l].astype(BF16), w_sh_down[l].astype(BF16)
    n1 = norm1_g[l].reshape(1, d)
    n2 = norm2_g[l].reshape(1, d)

    xp = x_prompt.reshape(t, d)
    (qtb, krb, ktf, vtf, vtb, mqk, mv, mo_raw, gates, gates_t) = _inproj(
        xp, modp[0], modp[1], n1, wqkv_t, wml_t, wg_t, b_gate)
    attn_p = _prompt_attention(qtb, krb, vtb, rel_bias, da_norm_g[l], lams, lam_init)
    mo_p, s_p, m_p = _mlstm_prompt(mqk, mv, mo_raw, gates, gates_t, conv_w[l], conv_b[l], ml_norm_g[l])

    xs = x_sample.reshape(bs, d)
    qkvt_s, mqk_s, mvot_s, gates_ts = _sample_inproj(xs, mods[0], mods[1], n1, wqkv_t, wml_t, wg_t, b_gate)
    conv_buf = jnp.transpose(state_conv[l], (1, 0, 2))
    qkt_s, conv_new_s = _sample_conv(mqk_s, conv_buf, conv_w[l], conv_b[l])
    c_t = jnp.transpose(state_C[l], (1, 2, 3, 0))
    n_t = jnp.transpose(state_n[l], (1, 2, 0))
    m_t = state_m[l].T
    c_s, n_s, m_s, mo_ts = _sample_mlstm(qkt_s, mvot_s[0:ML_COLS], mvot_s[ML_COLS:2 * ML_COLS], gates_ts, m_t, c_t,
                                         n_t, ml_norm_g[l])
    n_pool = cache_k.shape[1]
    ck_t = jnp.transpose(cache_k[l], (0, 2, 3, 1)).reshape(n_pool, c, PAGE_SIZE)
    cv_t = jnp.transpose(cache_v[l], (0, 2, 3, 1)).reshape(n_pool, c, PAGE_SIZE)
    attn_ts = _sample_attention(qkvt_s[0:c], qkvt_s[c:2 * c], qkvt_s[2 * c:3 * c], ck_t, cv_t,
                                page_table.T, rel_bias, da_norm_g[l], lams, lam_init)

    xs1_p, xs1_s, h2t, logits_t = _outproj(attn_p, mo_p, xp, modp[2:6], attn_ts, mo_ts, xs, mods[2:6], w_out_b, n2,
                                           w_router_t, wsg, wsu, wsd)

    n_e = w_router.shape[2]
    eidx, wts, rank, cnt = _route(logits_t, router_bias[l])
    counts = cnt[:, 0].astype(I32)
    padded = (counts + MOE_BM - 1) // MOE_BM * MOE_BM
    pad_end = jnp.cumsum(padded)
    offs = pad_end - padded
    n_blocks = -(-(t_all * TOP_K) // MOE_BM) + n_e
    block_e = jnp.minimum(jnp.searchsorted(pad_end, jnp.arange(n_blocks) * MOE_BM, side='right'),
                          n_e - 1).astype(I32)
    n_valid = (pad_end[-1:] // MOE_BM).astype(I32)
    pos = _positions(eidx, rank, offs)
    xs_sorted = _dispatch(pos, h2t, n_blocks * MOE_BM)
    ys = _experts(block_e, n_valid, xs_sorted, w_exp_gate[l], w_exp_up[l], w_exp_down[l], n_blocks)
    w_tm = wts.T
    y_p, y_s = _combine(pos, w_tm, xs1_p, modp[5], xs1_s, mods[5], final_norm_g, ys)

    def per_head(a_t, n_feat):
        return jnp.transpose(a_t.reshape(-1, n_feat, a_t.shape[1]), (2, 0, 1))

    return (
        y_p.reshape(bp, t, d),
        y_s.reshape(bs, ts, d),
        per_head(ktf, 2 * DA_QK_DIM).reshape(1, bp, t, DA_HEADS, 2 * DA_QK_DIM),
        per_head(vtf, DA_V_DIM).reshape(1, bp, t, DA_HEADS, DA_V_DIM),
        s_p[:, :, :hd].reshape(1, bp, ML_HEADS, hd, hd),
        s_p[:, :, hd].reshape(1, bp, ML_HEADS, hd),
        m_p.reshape(1, bp, ML_HEADS),
        mqk[t - (CONV_W - 1):, :].reshape(1, bp, CONV_W - 1, 2 * ML_COLS),
        per_head(qkvt_s[c:2 * c], 2 * DA_QK_DIM).reshape(1, bs, ts, DA_HEADS, 2 * DA_QK_DIM),
        per_head(qkvt_s[2 * c:3 * c], DA_V_DIM).reshape(1, bs, ts, DA_HEADS, DA_V_DIM),
        jnp.transpose(c_s, (3, 0, 1, 2)).reshape(1, bs, ML_HEADS, hd, hd),
        jnp.transpose(n_s, (2, 0, 1)).reshape(1, bs, ML_HEADS, hd),
        m_s.reshape(ML_HEADS, bs).T.reshape(1, bs, ML_HEADS),
        jnp.transpose(conv_new_s, (1, 0, 2)).reshape(1, bs, CONV_W - 1, 2 * ML_COLS),
    )
```

```python
import functools
import math

import jax
import jax.numpy as jnp
import numpy as np
from jax import lax
from jax.experimental import pallas as pl
from jax.experimental.pallas import tpu as pltpu

F32 = jnp.float32
BF16 = jnp.bfloat16
I32 = jnp.int32
HI = lax.Precision.HIGHEST

DA_HEADS = 8
DA_QK_DIM = 32
DA_V_DIM = 64
ML_HEADS = 8
ML_HEAD_DIM = 64
CONV_W = 4
NUM_BUCKETS = 32
MAX_DISTANCE = 128
PAGE_SIZE = 128
N_GROUPS = 8
TOPK_GROUPS = 4
TOP_K = 8
ROUTED_SCALE = 2.5
EPS = 1e-6

DA_COLS = DA_HEADS * 2 * DA_QK_DIM
ML_COLS = ML_HEADS * ML_HEAD_DIM
N_GATES = 2 * ML_HEADS
Q_SCALE = DA_QK_DIM ** -0.5
K_SCALE = ML_HEAD_DIM ** -0.5
LOG2E = 1.0 / math.log(2.0)
NEG = -1e30

LANES = 128
SUBLANES = 8
VMEM_LIMIT = 56 * 1024 * 1024

ML_CHUNK = 128
ATT_TILE = 1024
ATT_SUB = 1024
ATT_DEN_ROWS = 16
MOE_BM = 256
TOK_BLK = 128
PAGES_PER_STEP = 32
PROJ_TM = 256

NT = (((1,), (1,)), ((), ()))


def _const_spec(shape):
    nd = len(shape)
    return pl.BlockSpec(shape, lambda *_: (0,) * nd)


def _params(sem, vmem=VMEM_LIMIT):
    return pltpu.CompilerParams(dimension_semantics=sem, vmem_limit_bytes=vmem)


def _sigmoid(x):
    return 1.0 / (1.0 + jnp.exp(-x))


def _log_sigmoid(x):
    return jnp.minimum(x, 0.0) - jnp.log1p(jnp.exp(-jnp.abs(x)))


def _lam(lq1, lk1, lq2, lk2, lam_init):
    a = jnp.sum(lq1[...] * lk1[...], axis=-1, keepdims=True)
    b = jnp.sum(lq2[...] * lk2[...], axis=-1, keepdims=True)
    return jnp.exp(a) - jnp.exp(b) + lam_init


def _modulated_norm(x, g, shift, scale):
    y = x * lax.rsqrt(jnp.mean(x * x, axis=-1, keepdims=True) + EPS) * g
    return y * (1.0 + scale) + shift


def _t5_bucket(dist):
    n = jnp.maximum(dist, 0)
    max_exact = NUM_BUCKETS // 2
    nf = jnp.maximum(n, 1).astype(F32)
    large = max_exact + (jnp.log(nf / max_exact) / math.log(MAX_DISTANCE / max_exact)
                         * (NUM_BUCKETS - max_exact)).astype(I32)
    return jnp.where(n < max_exact, n, jnp.minimum(large, NUM_BUCKETS - 1))


def _bias_by_distance(rel_bias, n):
    rb = rel_bias.astype(F32)
    return rb[_t5_bucket(jnp.arange(n))] - rb[NUM_BUCKETS - 1][None, :]


def _adaln_kernel(c_ref, w_ref, b_ref, o_ref):
    c = c_ref[...]
    a = c * _sigmoid(c)
    o_ref[...] = jnp.dot(a, w_ref[...], precision=HI, preferred_element_type=F32) + b_ref[...]


def _adaln(c_all, w, b):
    r, d = c_all.shape
    n = w.shape[1]
    tn = n // 4
    return pl.pallas_call(
        _adaln_kernel,
        grid=(n // tn,),
        in_specs=[_const_spec((r, d)), pl.BlockSpec((d, tn), lambda j: (0, j)),
                  pl.BlockSpec((1, tn), lambda j: (0, j))],
        out_specs=pl.BlockSpec((r, tn), lambda j: (0, j)),
        out_shape=jax.ShapeDtypeStruct((r, n), F32),
        compiler_params=_params(("parallel",)),
        name="adaln",
    )(c_all, w, b.reshape(1, n))


def _inproj_kernel(x_ref, sh_ref, sc_ref, g_ref, wqkv_ref, wml_ref, wg_ref, bg_ref, bgt_ref,
                   qtb_ref, krb_ref, ktf_ref, vtf_ref, vtb_ref, mqk_ref, mv_ref, mo_ref, gates_ref, gatest_ref):
    h = _modulated_norm(x_ref[...], g_ref[...], sh_ref[...], sc_ref[...])
    hb = h.astype(BF16)
    c = DA_COLS
    qkvt = lax.dot_general(wqkv_ref[...], hb, NT, preferred_element_type=F32)
    qtb_ref[...] = (qkvt[0:c, :] * (Q_SCALE * LOG2E)).astype(BF16)
    ktf_ref[...] = qkvt[c:2 * c, :]
    vtf_ref[...] = qkvt[2 * c:3 * c, :]
    vtb_ref[...] = qkvt[2 * c:3 * c, :].astype(BF16)
    krb_ref[...] = lax.dot_general(hb, wqkv_ref[c:2 * c, :], NT, preferred_element_type=F32).astype(BF16)
    ml = lax.dot_general(hb, wml_ref[...], NT, preferred_element_type=F32)
    mqk_ref[...] = ml[:, 0:2 * ML_COLS]
    mv_ref[...] = ml[:, 2 * ML_COLS:3 * ML_COLS].astype(BF16)
    mo_ref[...] = ml[:, 3 * ML_COLS:4 * ML_COLS]
    graw = lax.dot_general(h, wg_ref[...], NT, precision=HI, preferred_element_type=F32) + bg_ref[...]
    lane = lax.broadcasted_iota(I32, graw.shape, 1)
    gates_ref[...] = jnp.where(lane < ML_HEADS, graw, _log_sigmoid(graw))
    grawt = lax.dot_general(wg_ref[...], h, NT, precision=HI, preferred_element_type=F32) + bgt_ref[...]
    sub = lax.broadcasted_iota(I32, grawt.shape, 0)
    gatest_ref[...] = jnp.where(sub < ML_HEADS, grawt, _log_sigmoid(grawt))


def _inproj(x, shift, scale, g, wqkv_t, wml_t, wg_t, b_gate):
    r, d = x.shape
    tm = min(PROJ_TM, r)
    row = lambda n: pl.BlockSpec((tm, n), lambda i: (i, 0))
    col = lambda n: pl.BlockSpec((n, tm), lambda i: (0, i))
    ng = N_GATES
    out_shape = [
        jax.ShapeDtypeStruct((DA_COLS, r), BF16), jax.ShapeDtypeStruct((r, DA_COLS), BF16),
        jax.ShapeDtypeStruct((DA_COLS, r), F32),
        jax.ShapeDtypeStruct((DA_COLS, r), F32), jax.ShapeDtypeStruct((DA_COLS, r), BF16),
        jax.ShapeDtypeStruct((r, 2 * ML_COLS), F32), jax.ShapeDtypeStruct((r, ML_COLS), BF16),
        jax.ShapeDtypeStruct((r, ML_COLS), F32),
        jax.ShapeDtypeStruct((r, ng), F32), jax.ShapeDtypeStruct((ng, r), F32),
    ]
    out_specs = [col(DA_COLS), row(DA_COLS), col(DA_COLS), col(DA_COLS), col(DA_COLS), row(2 * ML_COLS),
                 row(ML_COLS), row(ML_COLS), row(ng), col(ng)]
    return pl.pallas_call(
        _inproj_kernel,
        grid=(r // tm,),
        in_specs=[row(d), _const_spec((1, d)), _const_spec((1, d)), _const_spec((1, d)),
                  _const_spec(wqkv_t.shape), _const_spec(wml_t.shape),
                  _const_spec(wg_t.shape), _const_spec((1, ng)), _const_spec((ng, 1))],
        out_specs=out_specs,
        out_shape=out_shape,
        compiler_params=_params(("parallel",)),
        name="inproj",
    )(x, shift, scale, g, wqkv_t, wml_t, wg_t, b_gate.reshape(1, ng), b_gate.reshape(ng, 1))


def _pair_rmsnorm(x, lane, nd):
    sq = x * x
    lo = jnp.sum(jnp.where(lane < nd, sq, 0.0), axis=-1, keepdims=True) * (1.0 / nd)
    hi = jnp.sum(jnp.where(lane >= nd, sq, 0.0), axis=-1, keepdims=True) * (1.0 / nd)
    return x * jnp.where(lane < nd, lax.rsqrt(lo + EPS), lax.rsqrt(hi + EPS))


def _mlstm_kernel(mqk_ref, mv_ref, og_ref, gates_ref, gatest_ref, cw_ref, cb_ref, mlg_ref,
                  tri_ref, trit_ref, mo_ref, sout_ref, mout_ref, xprev, st, msc):
    c = pl.program_id(0)
    L = mqk_ref.shape[0]
    hd = ML_HEAD_DIM

    @pl.when(c == 0)
    def _():
        xprev[...] = jnp.zeros_like(xprev)
        st[...] = jnp.zeros_like(st)
        msc[...] = jnp.zeros_like(msc)

    x = mqk_ref[...]
    xp = xprev[...]
    row = lax.broadcasted_iota(I32, x.shape, 0)

    def shifted(j):
        return jnp.where(row < j, pltpu.roll(xp, j, 0), pltpu.roll(x, j, 0))

    y = cb_ref[...] + shifted(3) * cw_ref[0:1, :]
    y = y + shifted(2) * cw_ref[1:2, :]
    y = y + shifted(1) * cw_ref[2:3, :]
    y = y + x * cw_ref[3:4, :]
    qk = y * _sigmoid(y)
    xprev[...] = x

    g = gates_ref[...]
    gt = gatest_ref[...]
    fcol_all = jnp.dot(tri_ref[...], g, precision=HI, preferred_element_type=F32)
    frow_all = jnp.dot(gt, trit_ref[...], precision=HI, preferred_element_type=F32)
    r_i = lax.broadcasted_iota(I32, (L, L), 0)
    c_i = lax.broadcasted_iota(I32, (L, L), 1)
    causal = c_i <= r_i
    lane = lax.broadcasted_iota(I32, (L, 2 * hd), 1)

    for p in range(ML_HEADS // 2):
        qpair = qk[:, 2 * hd * p:2 * hd * (p + 1)]
        kpair = qk[:, ML_COLS + 2 * hd * p:ML_COLS + 2 * hd * (p + 1)] * K_SCALE
        vpair = mv_ref[:, 2 * hd * p:2 * hd * (p + 1)].astype(F32)
        hh, decays, vaugs, carries = [], [], [], []
        for j in range(2):
            h = 2 * p + j
            q_h = qpair[:, hd * j:hd * (j + 1)].astype(BF16)
            k_h = kpair[:, hd * j:hd * (j + 1)].astype(BF16)
            s_raw = lax.dot_general(q_h, k_h, NT, preferred_element_type=F32)
            fc = fcol_all[:, ML_HEADS + h:ML_HEADS + h + 1]
            fr = frow_all[ML_HEADS + h:ML_HEADS + h + 1, :]
            igr = gt[h:h + 1, :]
            igc = g[:, h:h + 1]
            m_old = msc[h]
            dlog = jnp.where(causal, fc - fr + igr, NEG)
            inter = m_old + fc
            m_t = jnp.maximum(inter, jnp.max(dlog, axis=-1, keepdims=True))
            dmat = jnp.where(causal, jnp.exp(dlog - m_t), 0.0)
            smat = (s_raw * dmat).astype(BF16)
            vsrc = vpair if j == 0 else pltpu.roll(vpair, hd, 1)
            vaug = jnp.where(lane < hd, vsrc, jnp.where(lane == hd, 1.0, 0.0)).astype(BF16)
            s_h = st[h]
            q_s = jnp.dot(q_h, s_h.astype(BF16), preferred_element_type=F32)
            num = jnp.dot(smat, vaug, preferred_element_type=F32) + jnp.exp(inter - m_t) * q_s
            den = num[:, hd:hd + 1]
            hh.append(num / jnp.maximum(jnp.abs(den), jnp.exp(-m_t)))
            m_new = m_t[L - 1:L, :]
            f_last = fc[L - 1:L, :]
            decays.append(jnp.exp(f_last - fc + igc - m_new))
            carries.append(jnp.exp(m_old + f_last - m_new))
            vaugs.append(vaug)
            msc[h] = m_new
        kd = kpair * jnp.where(lane < hd, decays[0], decays[1])
        kdt = kd.T
        for j in range(2):
            h = 2 * p + j
            upd = jnp.dot(kdt[hd * j:hd * (j + 1), :].astype(BF16), vaugs[j], preferred_element_type=F32)
            st[h] = carries[j] * st[h] + upd
        hpair = jnp.where(lane < hd, hh[0], pltpu.roll(hh[1], hd, 1))
        hn = _pair_rmsnorm(hpair, lane, hd) * mlg_ref[:, 2 * hd * p:2 * hd * (p + 1)]
        og = _sigmoid(og_ref[:, 2 * hd * p:2 * hd * (p + 1)])
        mo_ref[:, 2 * hd * p:2 * hd * (p + 1)] = (hn * og).astype(BF16)

    sout_ref[...] = st[...]
    mout_ref[...] = msc[...]


def _mlstm_prompt(mqk, mv, og, gates, gates_t, conv_w, conv_b, ml_g):
    t = mqk.shape[0]
    L = min(ML_CHUNK, t)
    hd = ML_HEAD_DIM
    tri = jnp.asarray(np.tril(np.ones((L, L), np.float32)))
    row = lambda n: pl.BlockSpec((L, n), lambda i: (i, 0))
    ng = N_GATES
    return pl.pallas_call(
        _mlstm_kernel,
        grid=(t // L,),
        in_specs=[row(2 * ML_COLS), row(ML_COLS), row(ML_COLS), row(ng),
                  pl.BlockSpec((ng, L), lambda i: (0, i)),
                  _const_spec((CONV_W, 2 * ML_COLS)), _const_spec((1, 2 * ML_COLS)),
                  _const_spec((1, ML_COLS)), _const_spec((L, L)), _const_spec((L, L))],
        out_specs=[row(ML_COLS), _const_spec((ML_HEADS, hd, 2 * hd)), _const_spec((ML_HEADS, 1, 1))],
        out_shape=[jax.ShapeDtypeStruct((t, ML_COLS), BF16),
                   jax.ShapeDtypeStruct((ML_HEADS, hd, 2 * hd), F32),
                   jax.ShapeDtypeStruct((ML_HEADS, 1, 1), F32)],
        scratch_shapes=[pltpu.VMEM((L, 2 * ML_COLS), F32), pltpu.VMEM((ML_HEADS, hd, 2 * hd), F32),
                        pltpu.VMEM((ML_HEADS, 1, 1), F32)],
        compiler_params=_params(("arbitrary",)),
        name="mlstm_prompt",
    )(mqk, mv, og, gates, gates_t, conv_w, conv_b.reshape(1, -1), ml_g.reshape(1, -1), tri, tri.T)


def _pattn_kernel(ii_ref, jj_ref, qt_ref, k_ref, vt_ref, bias_ref, dag_ref, lq1, lk1, lq2, lk2,
                  o_ref, m_sc, acc_sc, *, lam_init):
    s_id = pl.program_id(1)
    i = ii_ref[s_id]
    j = jj_ref[s_id]
    dq = DA_QK_DIM
    dv = DA_V_DIM

    @pl.when(j == 0)
    def _():
        m_sc[...] = jnp.full_like(m_sc, NEG)
        acc_sc[...] = jnp.zeros_like(acc_sc)

    def step(use_bias):
        qt = qt_ref[...]
        k = k_ref[...]
        vt = vt_ref[...]
        ones_rows = jnp.ones((ATT_DEN_ROWS, vt.shape[1]), BF16)
        vt_aug = tuple(jnp.concatenate([vt[hp * dv:(hp + 1) * dv, :], ones_rows], axis=0) for hp in range(2))
        tk = k.shape[0]
        sub = min(ATT_SUB, tk)
        work = [(idx, k0) for k0 in range(0, tk, sub) for idx in range(4)]

        def scores(idx, k0):
            c0 = idx * dq
            return jnp.dot(k[k0:k0 + sub, c0:c0 + dq], qt[c0:c0 + dq, :], preferred_element_type=F32)

        st_next = scores(*work[0])
        for n, (idx, k0) in enumerate(work):
            st = st_next
            if n + 1 < len(work):
                st_next = scores(*work[n + 1])
            hp = idx // 2
            if use_bias:
                st = st + bias_ref[hp, k0:k0 + sub, :]
            m_prev = m_sc[idx]
            m_new = jnp.maximum(m_prev, jnp.max(st, axis=0, keepdims=True))
            p = jnp.exp2(st - m_new)
            alpha = jnp.exp2(m_prev - m_new)
            m_sc[idx] = m_new
            acc_sc[idx] = acc_sc[idx] * alpha + jnp.dot(vt_aug[hp][:, k0:k0 + sub], p.astype(BF16),
                                                        preferred_element_type=F32)

    near = j >= i - 1
    pl.when(near)(lambda: step(True))
    pl.when(jnp.logical_not(near))(lambda: step(False))

    @pl.when(j == i)
    def _():
        lam = _lam(lq1, lk1, lq2, lk2, lam_init)
        for hp in range(2):
            a1 = acc_sc[2 * hp]
            a2 = acc_sc[2 * hp + 1]
            o = a1[0:dv, :] / a1[dv:dv + 1, :] - lam * (a2[0:dv, :] / a2[dv:dv + 1, :])
            o = o * lax.rsqrt(jnp.mean(o * o, axis=0, keepdims=True) + EPS)
            o_ref[hp * dv:(hp + 1) * dv, :] = o * dag_ref[hp * dv:(hp + 1) * dv, :] * (1.0 - lam_init)


def _prompt_bias_tiles(rel_bias, tile):
    rb = rel_bias.astype(F32)
    rb = (rb - rb[NUM_BUCKETS - 1][None, :]) * LOG2E
    kpos = jnp.arange(tile)[:, None]
    qpos = jnp.arange(tile)[None, :]
    dist = jnp.stack([tile + qpos - kpos, qpos - kpos], axis=0)
    bucket = _t5_bucket(dist)[None]
    tiles = jnp.zeros((DA_HEADS, 2, tile, tile), F32)
    for b in range(NUM_BUCKETS - 1):
        tiles = jnp.where(bucket == b, rb[b][:, None, None, None], tiles)
    tiles = jnp.where((dist >= 0)[None], tiles, NEG)
    return tiles.reshape(DA_HEADS // 2, 2, 2, tile, tile)


def _prompt_attention(qt_bf, k_bf, vt_bf, rel_bias, da_g, lams, lam_init):
    t = k_bf.shape[0]
    tile = min(ATT_TILE, t)
    assert tile > MAX_DISTANCE or tile == t
    nq = t // tile
    ii, jj = [], []
    for i in range(nq):
        for j in range(i + 1):
            ii.append(i)
            jj.append(j)
    ii = jnp.asarray(np.array(ii, np.int32))
    jj = jnp.asarray(np.array(jj, np.int32))
    bias = _prompt_bias_tiles(rel_bias, tile)
    w = 2 * DA_V_DIM
    lam_specs = [_const_spec((1, DA_QK_DIM))] * 4
    grid_spec = pltpu.PrefetchScalarGridSpec(
        num_scalar_prefetch=2,
        grid=(DA_HEADS // 2, ii.shape[0]),
        in_specs=[
            pl.BlockSpec((w, tile), lambda p, s, ii, jj: (p, ii[s])),
            pl.BlockSpec((tile, w), lambda p, s, ii, jj: (jj[s], p)),
            pl.BlockSpec((w, tile), lambda p, s, ii, jj: (p, jj[s])),
            pl.BlockSpec((None, 2, None, tile, tile),
                         lambda p, s, ii, jj: (p, 0, jnp.clip(jj[s] - ii[s] + 1, 0, 1), 0, 0)),
            pl.BlockSpec((w, 1), lambda p, s, ii, jj: (p, 0)),
        ] + lam_specs,
        out_specs=pl.BlockSpec((w, tile), lambda p, s, ii, jj: (p, ii[s])),
        scratch_shapes=[pltpu.VMEM((4, 1, tile), F32), pltpu.VMEM((4, DA_V_DIM + ATT_DEN_ROWS, tile), F32)],
    )
    return pl.pallas_call(
        functools.partial(_pattn_kernel, lam_init=lam_init),
        grid_spec=grid_spec,
        out_shape=jax.ShapeDtypeStruct((DA_COLS, t), F32),
        compiler_params=_params(("parallel", "arbitrary")),
        name="prompt_attention",
    )(ii, jj, qt_bf, k_bf, vt_bf, bias, da_g.reshape(-1, 1), *lams)


def _sinproj_kernel(x_ref, sh_ref, sc_ref, g_ref, wqkv_ref, wml_ref, wg_ref, bgt_ref,
                    qkvt_ref, mqk_ref, mvot_ref, gatest_ref):
    h = _modulated_norm(x_ref[...], g_ref[...], sh_ref[...], sc_ref[...])
    hb = h.astype(BF16)
    qkvt_ref[...] = lax.dot_general(wqkv_ref[...], hb, NT, preferred_element_type=F32)
    mqk_ref[...] = lax.dot_general(hb, wml_ref[0:2 * ML_COLS, :], NT, preferred_element_type=F32)
    mvot_ref[...] = lax.dot_general(wml_ref[2 * ML_COLS:4 * ML_COLS, :], hb, NT, preferred_element_type=F32)
    grawt = lax.dot_general(wg_ref[...], h, NT, precision=HI, preferred_element_type=F32) + bgt_ref[...]
    sub = lax.broadcasted_iota(I32, grawt.shape, 0)
    gatest_ref[...] = jnp.where(sub < ML_HEADS, grawt, _log_sigmoid(grawt))


def _sample_inproj(x, shift, scale, g, wqkv_t, wml_t, wg_t, b_gate):
    b, d = x.shape
    return pl.pallas_call(
        _sinproj_kernel,
        out_shape=[jax.ShapeDtypeStruct((3 * DA_COLS, b), F32), jax.ShapeDtypeStruct((b, 2 * ML_COLS), F32),
                   jax.ShapeDtypeStruct((2 * ML_COLS, b), F32), jax.ShapeDtypeStruct((N_GATES, b), F32)],
        compiler_params=pltpu.CompilerParams(vmem_limit_bytes=VMEM_LIMIT),
        name="sample_inproj",
    )(x, shift, scale, g, wqkv_t, wml_t, wg_t, b_gate.reshape(N_GATES, 1))


def _sconv_kernel(mqk_ref, buf_ref, cw_ref, cb_ref, qkt_ref, new_ref):
    x = mqk_ref[...]
    y = cb_ref[...] + buf_ref[0] * cw_ref[0:1, :]
    y = y + buf_ref[1] * cw_ref[1:2, :]
    y = y + buf_ref[2] * cw_ref[2:3, :]
    y = y + x * cw_ref[3:4, :]
    a = y * _sigmoid(y)
    lane = lax.broadcasted_iota(I32, a.shape, 1)
    qkt_ref[...] = jnp.where(lane < ML_COLS, a, a * K_SCALE).T
    new_ref[0] = buf_ref[1]
    new_ref[1] = buf_ref[2]
    new_ref[2] = x


def _sample_conv(mqk, conv_buf, conv_w, conv_b):
    b, n = mqk.shape
    return pl.pallas_call(
        _sconv_kernel,
        out_shape=[jax.ShapeDtypeStruct((n, b), F32), jax.ShapeDtypeStruct(conv_buf.shape, F32)],
        compiler_params=pltpu.CompilerParams(vmem_limit_bytes=VMEM_LIMIT),
        name="sample_conv",
    )(mqk, conv_buf, conv_w, conv_b.reshape(1, -1))


def _smlstm_kernel(q_ref, k_ref, v_ref, og_ref, ig_ref, lf_ref, m_ref, c_ref, n_ref, mlg_ref,
                   co_ref, no_ref, mo_ref, ho_ref):
    hd = ML_HEAD_DIM
    ig = ig_ref[...]
    lf = lf_ref[...]
    m_old = m_ref[...]
    m_t = jnp.maximum(m_old + lf, ig)
    a = jnp.exp(ig - m_t)
    w = jnp.exp(m_old + lf - m_t)
    v = v_ref[...]

    def body(d, num):
        cn = w * c_ref[d] + (a * k_ref[pl.ds(d, 1), :]) * v
        co_ref[d] = cn
        return num + q_ref[pl.ds(d, 1), :] * cn

    num = lax.fori_loop(0, hd, body, jnp.zeros(v.shape, F32))
    nn = w * n_ref[...] + a * k_ref[...]
    den = jnp.sum(q_ref[...] * nn, axis=0, keepdims=True)
    hv = num / jnp.maximum(jnp.abs(den), jnp.exp(-m_t))
    hn = hv * lax.rsqrt(jnp.mean(hv * hv, axis=0, keepdims=True) + EPS) * mlg_ref[...]
    no_ref[...] = nn
    mo_ref[...] = m_t
    ho_ref[...] = hn * _sigmoid(og_ref[...])


def _sample_mlstm(qk_t, v_t, og_t, gates_t, m_t, c_t, n_t, ml_g):
    b = qk_t.shape[1]
    h, d = ML_HEADS, ML_HEAD_DIM
    head_rows = lambda off: pl.BlockSpec((d, b), lambda i: (i + off, 0))
    vec = pl.BlockSpec((None, 1, b), lambda i: (i, 0, 0))
    return pl.pallas_call(
        _smlstm_kernel,
        grid=(h,),
        in_specs=[head_rows(0), head_rows(h), head_rows(0), head_rows(0),
                  vec, pl.BlockSpec((None, 1, b), lambda i: (i + h, 0, 0)), vec,
                  pl.BlockSpec((None, d, d, b), lambda i: (i, 0, 0, 0)),
                  pl.BlockSpec((None, d, b), lambda i: (i, 0, 0)),
                  pl.BlockSpec((None, d, 1), lambda i: (i, 0, 0))],
        out_specs=[pl.BlockSpec((None, d, d, b), lambda i: (i, 0, 0, 0)),
                   pl.BlockSpec((None, d, b), lambda i: (i, 0, 0)), vec, head_rows(0)],
        out_shape=[jax.ShapeDtypeStruct((h, d, d, b), F32), jax.ShapeDtypeStruct((h, d, b), F32),
                   jax.ShapeDtypeStruct((h, 1, b), F32), jax.ShapeDtypeStruct((h * d, b), F32)],
        compiler_params=_params(("parallel",)),
        name="sample_mlstm",
    )(qk_t, qk_t, v_t, og_t, gates_t.reshape(2 * h, 1, b), gates_t.reshape(2 * h, 1, b),
      m_t.reshape(h, 1, b), c_t, n_t, ml_g.reshape(h, d, 1))


def _sattn_kernel(pt_ref, qt_ref, knt_ref, vnt_ref, bpage_ref, bnew_ref, dag_ref, lq1, lk1, lq2, lk2,
                  ck_ref, cv_ref, o_ref, s_sc, a_sc, acc, qb, kbuf, vbuf, ksem, vsem, *, n_pg, lam_init):
    b = pl.program_id(0)
    ph = pl.program_id(1)
    g = pl.program_id(2)
    n_b = pl.num_programs(0)
    n_g = pl.num_programs(2)
    dq = DA_QK_DIM
    dv = DA_V_DIM
    nh = DA_HEADS
    n_pages = s_sc.shape[0] - 1
    lane = lax.broadcasted_iota(I32, qt_ref.shape, 1)

    def group_copies(cache_ref, buf, sem, bq, gq, slot):
        return [pltpu.make_async_copy(cache_ref.at[pt_ref[gq * n_pg + i, bq]], buf.at[slot, i], sem.at[slot])
                for i in range(n_pg)]

    def start_group(phq, bq, gq):
        slot = (bq * n_g + gq) % 2

        @pl.when(phq == 0)
        def _():
            for cp in group_copies(ck_ref, kbuf, ksem, bq, gq, slot):
                cp.start()

        @pl.when(phq == 1)
        def _():
            for cp in group_copies(cv_ref, vbuf, vsem, bq, gq, slot):
                cp.start()

    step = (b * 2 + ph) * n_g + g
    slot = (b * n_g + g) % 2

    @pl.when(step == 0)
    def _():
        start_group(ph, b, g)

    @pl.when(step + 1 < n_b * 2 * n_g)
    def _():
        nxt = step + 1
        start_group((nxt // n_g) % 2, nxt // (2 * n_g), nxt % n_g)

    k_refs = [kbuf.at[slot, i] for i in range(n_pg)]
    v_refs = [vbuf.at[slot, i] for i in range(n_pg)]

    def column(ref):
        return jnp.sum(jnp.where(lane == b, ref[...], 0.0), axis=1, keepdims=True)

    @pl.when((b == 0) & (ph == 0) & (g == 0))
    def _():
        o_ref[...] = jnp.zeros_like(o_ref)

    @pl.when((ph == 0) & (g == 0))
    def _():
        qb[...] = jnp.broadcast_to(column(qt_ref) * Q_SCALE, qb.shape)

    @pl.when(ph == 0)
    def _():
        for cp in group_copies(ck_ref, kbuf, ksem, b, g, slot):
            cp.wait()
        for c in range(2 * nh):
            qc = qb[c * dq:(c + 1) * dq, :]
            srow = (c % 2) * nh + c // 2
            for i in range(n_pg):
                prod = k_refs[i][c * dq:(c + 1) * dq, :] * qc
                s_sc[g * n_pg + i, srow:srow + 1, :] = jnp.sum(prod, axis=0, keepdims=True)

        @pl.when(g == n_g - 1)
        def _():
            s_sc[n_pages - 1] = s_sc[n_pages - 1] + bpage_ref[...]
            prodn = column(knt_ref) * qb[:, 0:1]
            sub = lax.broadcasted_iota(I32, (2 * nh, 1), 0)
            ln = bnew_ref[...]
            for r in range(2 * nh):
                c = (r % nh) * 2 + r // nh
                ln = ln + jnp.where(sub == r, jnp.sum(prodn[c * dq:(c + 1) * dq, :], axis=0, keepdims=True), 0.0)
            lane_s = lax.broadcasted_iota(I32, (2 * nh, LANES), 1)
            s_sc[n_pages] = jnp.where(lane_s == 0, ln, NEG)
            s_all = s_sc[...]
            mx = jnp.max(jnp.max(s_all, axis=0), axis=1, keepdims=True)
            p = jnp.exp(s_all - mx)
            l = jnp.sum(jnp.sum(p, axis=0), axis=1, keepdims=True)
            lam = _lam(lq1, lk1, lq2, lk2, lam_init)
            a_sc[...] = p[:, 0:nh, :] / l[0:nh, :] - lam * (p[:, nh:2 * nh, :] / l[nh:2 * nh, :])
            acc[...] = jnp.zeros_like(acc)

    @pl.when(ph == 1)
    def _():
        for cp in group_copies(cv_ref, vbuf, vsem, b, g, slot):
            cp.wait()
        for h in range(nh):
            part = acc[h * dv:(h + 1) * dv, :]
            for i in range(n_pg):
                part = part + v_refs[i][h * dv:(h + 1) * dv, :] * a_sc[g * n_pg + i, h:h + 1, :]
            acc[h * dv:(h + 1) * dv, :] = part

        @pl.when(g == n_g - 1)
        def _():
            out = jnp.sum(acc[...], axis=1, keepdims=True)
            vn = column(vnt_ref)
            a_new = a_sc[n_pages][:, 0:1]
            pieces = []
            for h in range(nh):
                oh = out[h * dv:(h + 1) * dv, :] + a_new[h:h + 1, :] * vn[h * dv:(h + 1) * dv, :]
                pieces.append(oh * lax.rsqrt(jnp.mean(oh * oh, axis=0, keepdims=True) + EPS))
            res = jnp.concatenate(pieces, axis=0) * dag_ref[...] * (1.0 - lam_init)
            o_ref[...] = jnp.where(lane == b, res, o_ref[...])


def _sample_attention(q_t, kn_t, vn_t, ck_t, cv_t, page_table_t, rel_bias, da_g, lams, lam_init):
    w, b = q_t.shape
    n_pages = page_table_t.shape[0]
    n_pg = min(PAGES_PER_STEP, n_pages)
    n_g = n_pages // n_pg
    ps = PAGE_SIZE
    past = n_pages * ps
    nh = DA_HEADS
    tab = _bias_by_distance(rel_bias, ps + 1)
    bpage = jnp.tile(tab[ps - jnp.arange(ps)].T, (2, 1))
    bnew = jnp.tile(tab[0:1].T, (2, 1))

    in_specs = [_const_spec((w, b))] * 3 + [_const_spec((2 * nh, ps)), _const_spec((2 * nh, 1)),
                                            _const_spec((w, 1))] + [_const_spec((1, DA_QK_DIM))] * 4
    in_specs += [pl.BlockSpec(memory_space=pl.ANY)] * 2
    grid_spec = pltpu.PrefetchScalarGridSpec(
        num_scalar_prefetch=1,
        grid=(b, 2, n_g),
        in_specs=in_specs,
        out_specs=_const_spec((w, b)),
        scratch_shapes=[pltpu.VMEM((n_pages + 1, 2 * nh, ps), F32), pltpu.VMEM((n_pages + 1, nh, ps), F32),
                        pltpu.VMEM((w, ps), F32), pltpu.VMEM((w, ps), F32),
                        pltpu.VMEM((2, n_pg, w, ps), F32), pltpu.VMEM((2, n_pg, w, ps), F32),
                        pltpu.SemaphoreType.DMA((2,)), pltpu.SemaphoreType.DMA((2,))],
    )
    return pl.pallas_call(
        functools.partial(_sattn_kernel, n_pg=n_pg, lam_init=lam_init),
        grid_spec=grid_spec,
        out_shape=jax.ShapeDtypeStruct((w, b), F32),
        compiler_params=_params(("arbitrary", "arbitrary", "arbitrary")),
        name="sample_attention",
    )(page_table_t, q_t, kn_t, vn_t, bpage, bnew, da_g.reshape(w, 1), *lams, ck_t, cv_t)


def _outproj_kernel(ap_ref, mop_ref, xp_ref, g1p_ref, sh2p_ref, sc2p_ref, g2p_ref,
                    as_ref, mos_ref, xs_ref, g1s_ref, sh2s_ref, sc2s_ref, g2s_ref,
                    wo_ref, n2_ref, wrt_ref, wsg_ref, wsu_ref, wsd_ref,
                    x1p_ref, x1s_ref, h2t_ref, lt_ref):
    i = pl.program_id(0)
    n_prompt = pl.num_programs(0) - 1

    def block(a, mo, x, g1, sh2, sc2, g2, out_ref):
        half = a.shape[1]
        mix = jnp.dot(a, wo_ref[0:half, :], preferred_element_type=F32)
        mix = mix + jnp.dot(mo, wo_ref[half:2 * half, :], preferred_element_type=F32)
        x1 = x + g1 * mix
        h2 = _modulated_norm(x1, n2_ref[...], sh2, sc2)
        lt_ref[...] = lax.dot_general(wrt_ref[...], h2, NT, precision=HI, preferred_element_type=F32)
        tm = h2.shape[0]
        for s in range(h2.shape[1] // LANES):
            h2t_ref[pl.ds(s, tm, stride=SUBLANES), :] = h2[:, s * LANES:(s + 1) * LANES]
        hb = h2.astype(BF16)
        gate = jnp.dot(hb, wsg_ref[...], preferred_element_type=F32)
        up = jnp.dot(hb, wsu_ref[...], preferred_element_type=F32)
        act = (gate * _sigmoid(gate) * up).astype(BF16)
        shared = jnp.dot(act, wsd_ref[...], preferred_element_type=F32)
        out_ref[...] = x1 + g2 * shared

    @pl.when(i < n_prompt)
    def _():
        block(ap_ref[...].T.astype(BF16), mop_ref[...], xp_ref[...], g1p_ref[...], sh2p_ref[...], sc2p_ref[...],
              g2p_ref[...], x1p_ref)

    @pl.when(i == n_prompt)
    def _():
        block(as_ref[...].T.astype(BF16), mos_ref[...].T.astype(BF16), xs_ref[...], g1s_ref[...], sh2s_ref[...],
              sc2s_ref[...], g2s_ref[...], x1s_ref)


def _outproj(a_p, mo_p, x_p, mod_p, a_ts, mo_ts, x_s, mod_s, w_out, n2, w_router_t, wsg, wsu, wsd):
    t, d = x_p.shape
    bs = x_s.shape[0]
    tm = TOK_BLK
    assert bs == tm and t % tm == 0
    nbp = t // tm
    e = w_router_t.shape[0]
    nsl = d // LANES
    prow = lambda n: pl.BlockSpec((tm, n), lambda i: (jnp.minimum(i, nbp - 1), 0))
    pcol = pl.BlockSpec((DA_COLS, tm), lambda i: (0, jnp.minimum(i, nbp - 1)))
    in_specs = ([pcol, prow(ML_COLS), prow(d)] + [_const_spec((1, d))] * 4
                + [_const_spec((DA_COLS, bs)), _const_spec((ML_COLS, bs)), _const_spec((bs, d))]
                + [_const_spec((bs, d))] * 4
                + [_const_spec(w_out.shape), _const_spec((1, d)), _const_spec(w_router_t.shape),
                   _const_spec(wsg.shape), _const_spec(wsu.shape), _const_spec(wsd.shape)])
    return pl.pallas_call(
        _outproj_kernel,
        grid=(nbp + 1,),
        in_specs=in_specs,
        out_specs=[prow(d), _const_spec((bs, d)), pl.BlockSpec((tm * nsl, LANES), lambda i: (i, 0)),
                   pl.BlockSpec((e, tm), lambda i: (0, i))],
        out_shape=[jax.ShapeDtypeStruct((t, d), F32), jax.ShapeDtypeStruct((bs, d), F32),
                   jax.ShapeDtypeStruct(((t + bs) * nsl, LANES), F32), jax.ShapeDtypeStruct((e, t + bs), F32)],
        compiler_params=_params(("arbitrary",)),
        name="outproj",
    )(a_p, mo_p, x_p, *mod_p, a_ts, mo_ts, x_s, *mod_s, w_out, n2, w_router_t, wsg, wsu, wsd)


def _first_max(cur, iota, n, axis=0):
    t = jnp.max(cur, axis=axis, keepdims=True)
    idx = jnp.min(jnp.where(cur == t, iota, float(n)), axis=axis, keepdims=True)
    return t, idx


def _route_kernel(lt_ref, bias_ref, ut_ref, ones_ref, eidx_ref, w_ref, rank_ref, cnt_ref, carry):
    i = pl.program_id(0)

    @pl.when(i == 0)
    def _():
        carry[...] = jnp.zeros_like(carry)

    aff = _sigmoid(lt_ref[...])
    sel = aff + bias_ref[...]
    n_e, n_t = aff.shape
    gsz = n_e // N_GROUPS
    e_iota = lax.broadcasted_iota(I32, (n_e, n_t), 0).astype(F32)
    g_iota = lax.broadcasted_iota(I32, (N_GROUPS, n_t), 0).astype(F32)
    in_iota = lax.broadcasted_iota(I32, (gsz, n_t), 0).astype(F32)
    ninf = -jnp.inf
    gscore = jnp.zeros((N_GROUPS, n_t), F32)
    for g in range(N_GROUPS):
        blk = sel[g * gsz:(g + 1) * gsz, :]
        t1, i1 = _first_max(blk, in_iota, gsz)
        t2 = jnp.max(jnp.where(in_iota == i1, ninf, blk), axis=0, keepdims=True)
        gscore = jnp.where(g_iota == float(g), t1 + t2, gscore)
    gmask = jnp.zeros((N_GROUPS, n_t), F32)
    cur = gscore
    for _ in range(TOPK_GROUPS):
        _, gi = _first_max(cur, g_iota, N_GROUPS)
        hit = g_iota == gi
        gmask = jnp.where(hit, 1.0, gmask)
        cur = jnp.where(hit, ninf, cur)
    cur = jnp.concatenate(
        [jnp.where(gmask[g:g + 1, :] > 0.0, sel[g * gsz:(g + 1) * gsz, :], ninf) for g in range(N_GROUPS)], axis=0)
    hits, ws = [], []
    msum = jnp.zeros((n_e, n_t), F32)
    for k in range(TOP_K):
        _, ei = _first_max(cur, e_iota, n_e)
        hit = e_iota == ei
        ws.append(jnp.sum(jnp.where(hit, aff, 0.0), axis=0, keepdims=True))
        cur = jnp.where(hit, ninf, cur)
        msum = msum + hit.astype(F32)
        hits.append(hit)
        eidx_ref[k:k + 1, :] = ei.astype(I32)
    wsum = ws[0]
    for k in range(1, TOP_K):
        wsum = wsum + ws[k]
    mb = msum.astype(BF16)
    base = carry[...] + jnp.dot(mb, ut_ref[...], preferred_element_type=F32)
    for k in range(TOP_K):
        w_ref[k:k + 1, :] = ws[k] / wsum * ROUTED_SCALE
        rank_ref[k:k + 1, :] = jnp.sum(jnp.where(hits[k], base, 0.0), axis=0, keepdims=True).astype(I32)
    carry[...] = carry[...] + jnp.dot(mb, ones_ref[...], preferred_element_type=F32)
    cnt_ref[...] = carry[...]


def _route(logits_t, router_bias):
    n_e, t_all = logits_t.shape
    tb = TOK_BLK
    ut = jnp.asarray(np.triu(np.ones((tb, tb), np.float32), 1), dtype=BF16)
    ones = jnp.ones((tb, tb), BF16)
    bias_b = jnp.broadcast_to(router_bias.astype(F32)[:, None], (n_e, tb))
    tok = pl.BlockSpec((TOP_K, tb), lambda i: (0, i))
    return pl.pallas_call(
        _route_kernel,
        grid=(t_all // tb,),
        in_specs=[pl.BlockSpec((n_e, tb), lambda i: (0, i)), _const_spec((n_e, tb)), _const_spec((tb, tb)),
                  _const_spec((tb, tb))],
        out_specs=[tok, tok, tok, _const_spec((n_e, tb))],
        out_shape=[jax.ShapeDtypeStruct((TOP_K, t_all), I32), jax.ShapeDtypeStruct((TOP_K, t_all), F32),
                   jax.ShapeDtypeStruct((TOP_K, t_all), I32), jax.ShapeDtypeStruct((n_e, tb), F32)],
        scratch_shapes=[pltpu.VMEM((n_e, tb), F32)],
        compiler_params=_params(("arbitrary",)),
        name="moe_route",
    )(logits_t, bias_b, ut, ones)


def _pos_kernel(eidx_ref, rank_ref, offs_ref, pos_ref):
    offs = offs_ref[...]
    e_iota = lax.broadcasted_iota(I32, offs.shape, 0)
    for k in range(TOP_K):
        hit = e_iota == eidx_ref[k:k + 1, :]
        base = jnp.sum(jnp.where(hit, offs, 0.0), axis=0, keepdims=True).astype(I32)
        pos_ref[k:k + 1, :] = base + rank_ref[k:k + 1, :]


def _positions(eidx, rank, offs):
    t_all = eidx.shape[1]
    tb = TOK_BLK
    n_e = offs.shape[0]
    tok = pl.BlockSpec((TOP_K, tb), lambda i: (0, i))
    return pl.pallas_call(
        _pos_kernel,
        grid=(t_all // tb,),
        in_specs=[tok, tok, _const_spec((n_e, tb))],
        out_specs=tok,
        out_shape=jax.ShapeDtypeStruct((TOP_K, t_all), I32),
        compiler_params=_params(("parallel",)),
        name="moe_positions",
    )(eidx, rank, jnp.broadcast_to(offs.astype(F32)[:, None], (n_e, tb)))


def _tile_start(row):
    start = row * SUBLANES
    return start if isinstance(start, int) else pl.multiple_of(start, SUBLANES)


def _row_copy(src, src_row, dst, dst_row, sem):
    return pltpu.make_async_copy(src.at[pl.ds(_tile_start(src_row), SUBLANES)],
                                 dst.at[pl.ds(_tile_start(dst_row), SUBLANES)], sem)


def _dispatch_kernel(pos_ref, h2t_ref, xs_in_ref, xs_ref, sem):
    del xs_in_ref
    tb = pos_ref.shape[1]

    def issue(t, carry):
        for k in range(TOP_K):
            _row_copy(h2t_ref, t, xs_ref, pos_ref[k, t], sem).start(priority=k % 2)
        return carry

    lax.fori_loop(0, tb, issue, 0)

    def drain(t, carry):
        for k in range(TOP_K):
            _row_copy(h2t_ref, 0, xs_ref, 0, sem).wait()
        return carry

    lax.fori_loop(0, tb, drain, 0)


def _dispatch(pos, h2t, n_rows):
    t_all = pos.shape[1]
    tb = TOK_BLK
    xs0 = jnp.zeros((n_rows * SUBLANES, LANES), F32)
    return pl.pallas_call(
        _dispatch_kernel,
        grid=(t_all // tb,),
        in_specs=[pl.BlockSpec((TOP_K, tb), lambda i: (0, i), memory_space=pltpu.SMEM),
                  pl.BlockSpec((tb * SUBLANES, LANES), lambda i: (i, 0)), pl.BlockSpec(memory_space=pl.ANY)],
        out_specs=pl.BlockSpec(memory_space=pl.ANY),
        out_shape=jax.ShapeDtypeStruct(xs0.shape, F32),
        input_output_aliases={2: 0},
        scratch_shapes=[pltpu.SemaphoreType.DMA(())],
        compiler_params=_params(("arbitrary",)),
        name="moe_dispatch",
    )(pos, h2t, xs0)


def _experts_kernel(be_ref, nv_ref, xs_ref, wg_ref, wu_ref, wd_ref, ys_ref, wgb, wub, wdb):
    i = pl.program_id(0)

    @pl.when(i < nv_ref[0])
    def _():
        @pl.when((i == 0) | (be_ref[i] != be_ref[jnp.maximum(i - 1, 0)]))
        def _():
            wgb[...] = wg_ref[...].astype(BF16)
            wub[...] = wu_ref[...].astype(BF16)
            wdb[...] = wd_ref[...].astype(BF16)

        bm = xs_ref.shape[0] // SUBLANES
        nsl = wg_ref.shape[0] // LANES
        x = jnp.concatenate([xs_ref[pl.ds(s, bm, stride=SUBLANES), :] for s in range(nsl)], axis=1).astype(BF16)
        gate = jnp.dot(x, wgb[...], preferred_element_type=F32)
        up = jnp.dot(x, wub[...], preferred_element_type=F32)
        act = (gate * _sigmoid(gate) * up).astype(BF16)
        y = jnp.dot(act, wdb[...], preferred_element_type=F32)
        for s in range(nsl):
            ys_ref[pl.ds(s, bm, stride=SUBLANES), :] = y[:, s * LANES:(s + 1) * LANES]


def _experts(block_e, n_valid, xs, w_g, w_u, w_d, n_blocks):
    bm = MOE_BM
    d, de = w_g.shape[1], w_g.shape[2]
    rows = pl.BlockSpec((bm * SUBLANES, LANES), lambda i, be, nv: (jnp.minimum(i, nv[0] - 1), 0))
    grid_spec = pltpu.PrefetchScalarGridSpec(
        num_scalar_prefetch=2,
        grid=(n_blocks,),
        in_specs=[rows,
                  pl.BlockSpec((None, d, de), lambda i, be, nv: (be[i], 0, 0)),
                  pl.BlockSpec((None, d, de), lambda i, be, nv: (be[i], 0, 0)),
                  pl.BlockSpec((None, de, d), lambda i, be, nv: (be[i], 0, 0))],
        out_specs=rows,
        scratch_shapes=[pltpu.VMEM((d, de), BF16), pltpu.VMEM((d, de), BF16), pltpu.VMEM((de, d), BF16)],
    )
    return pl.pallas_call(
        _experts_kernel,
        grid_spec=grid_spec,
        out_shape=jax.ShapeDtypeStruct(xs.shape, F32),
        input_output_aliases={2: 0},
        compiler_params=_params(("arbitrary",)),
        name="moe_experts",
    )(block_e, n_valid, xs, w_g, w_u, w_d)


def _combine_kernel(pos_ref, w_ref, x1p_ref, g2p_ref, x1s_ref, g2s_ref, fg_ref, ys_ref, yp_ref, ysm_ref, buf, sem):
    i = pl.program_id(0)
    n_prompt = pl.num_programs(0) - 1
    tb = pos_ref.shape[1]
    nsl = x1p_ref.shape[1] // LANES

    def issue(t, carry):
        for k in range(TOP_K):
            pltpu.make_async_copy(ys_ref.at[pl.ds(_tile_start(pos_ref[k, t]), SUBLANES)],
                                  buf.at[k, pl.ds(_tile_start(t), SUBLANES)], sem).start(priority=k % 2)
        return carry

    lax.fori_loop(0, tb, issue, 0)

    def drain(t, carry):
        for k in range(TOP_K):
            pltpu.make_async_copy(ys_ref.at[pl.ds(0, SUBLANES)], buf.at[k, pl.ds(0, SUBLANES)], sem).wait()
        return carry

    lax.fori_loop(0, tb, drain, 0)

    w = w_ref[...]
    routed = None
    for k in range(TOP_K):
        yk = jnp.concatenate([buf[k, pl.ds(s, tb, stride=SUBLANES), :] for s in range(nsl)], axis=1)
        term = yk * w[:, k:k + 1]
        routed = term if routed is None else routed + term

    def finish(x1, g2, out_ref):
        x2 = x1 + g2 * routed
        out_ref[...] = x2 * lax.rsqrt(jnp.mean(x2 * x2, axis=-1, keepdims=True) + EPS) * fg_ref[...]

    @pl.when(i < n_prompt)
    def _():
        finish(x1p_ref[...], g2p_ref[...], yp_ref)

    @pl.when(i == n_prompt)
    def _():
        finish(x1s_ref[...], g2s_ref[...], ysm_ref)


def _combine(pos, w_tm, x1_p, g2_p, x1_s, g2_s, final_g, ys):
    t, d = x1_p.shape
    bs = x1_s.shape[0]
    tb = TOK_BLK
    assert bs == tb and t % tb == 0
    nbp = t // tb
    prow = pl.BlockSpec((tb, d), lambda i: (jnp.minimum(i, nbp - 1), 0))
    return pl.pallas_call(
        _combine_kernel,
        grid=(nbp + 1,),
        in_specs=[pl.BlockSpec((TOP_K, tb), lambda i: (0, i), memory_space=pltpu.SMEM),
                  pl.BlockSpec((tb, TOP_K), lambda i: (i, 0)),
                  prow, _const_spec((1, d)), _const_spec((bs, d)), _const_spec((bs, d)), _const_spec((1, d)),
                  pl.BlockSpec(memory_space=pl.ANY)],
        out_specs=[prow, _const_spec((bs, d))],
        out_shape=[jax.ShapeDtypeStruct((t, d), F32), jax.ShapeDtypeStruct((bs, d), F32)],
        scratch_shapes=[pltpu.VMEM((TOP_K, tb * SUBLANES, LANES), F32), pltpu.SemaphoreType.DMA(())],
        compiler_params=_params(("arbitrary",)),
        name="moe_combine",
    )(pos, w_tm, x1_p, g2_p, x1_s, g2_s, final_g.reshape(1, d), ys)


def kernel(x_prompt, x_sample, cache_k, cache_v, state_C, state_n, state_m, state_conv, page_table, c_prompt, c_sample, w_ada, b_ada, norm1_g, norm2_g, w_in, conv_w, conv_b, lambda_q1, lambda_k1, lambda_q2, lambda_k2, da_norm_g, rel_bias, ml_b_i, ml_b_f, ml_norm_g, w_out, w_router, router_bias, w_exp_gate, w_exp_up, w_exp_down, w_sh_gate, w_sh_up, w_sh_down, final_norm_g):
    bp, t, d = x_prompt.shape
    bs, ts, _ = x_sample.shape
    depth = w_ada.shape[0]
    assert bp == 1 and ts == 1 and depth == 1
    assert d % LANES == 0 and (t + bs) % TOK_BLK == 0 and t % TOK_BLK == 0
    l = 0
    lam_init = 0.8 - 0.6 * math.exp(-0.3 * l)
    lams = [a[l].reshape(1, -1).astype(F32) for a in (lambda_q1, lambda_k1, lambda_q2, lambda_k2)]
    t_all = t + bs
    hd = ML_HEAD_DIM

    pad = (-(bp + bs)) % SUBLANES
    c_all = jnp.concatenate([c_prompt, c_sample, jnp.zeros((pad, d), F32)], axis=0)
    mod = _adaln(c_all, w_ada[l], b_ada[l])
    modp = [mod[0:1, i * d:(i + 1) * d] for i in range(6)]
    mods = [mod[1:1 + bs, i * d:(i + 1) * d] for i in range(6)]

    w_in_t = w_in[l].T
    c = DA_COLS
    wqkv_t = w_in_t[0:3 * c].astype(BF16)
    wml_t = w_in_t[3 * c:3 * c + 4 * ML_COLS].astype(BF16)
    wg_t = w_in_t[3 * c + 4 * ML_COLS:]
    b_gate = jnp.concatenate([ml_b_i[l], ml_b_f[l]]).astype(F32)
    w_out_b = w_out[l].astype(BF16)
    w_router_t = w_router[l].T
    wsg, wsu, wsd = w_sh_gate[l].astype(BF16), w_sh_up[l].astype(BF16), w_sh_down[l].astype(BF16)
    n1 = norm1_g[l].reshape(1, d)
    n2 = norm2_g[l].reshape(1, d)

    xp = x_prompt.reshape(t, d)
    (qtb, krb, ktf, vtf, vtb, mqk, mv, mo_raw, gates, gates_t) = _inproj(
        xp, modp[0], modp[1], n1, wqkv_t, wml_t, wg_t, b_gate)
    attn_p = _prompt_attention(qtb, krb, vtb, rel_bias, da_norm_g[l], lams, lam_init)
    mo_p, s_p, m_p = _mlstm_prompt(mqk, mv, mo_raw, gates, gates_t, conv_w[l], conv_b[l], ml_norm_g[l])

    xs = x_sample.reshape(bs, d)
    qkvt_s, mqk_s, mvot_s, gates_ts = _sample_inproj(xs, mods[0], mods[1], n1, wqkv_t, wml_t, wg_t, b_gate)
    conv_buf = jnp.transpose(state_conv[l], (1, 0, 2))
    qkt_s, conv_new_s = _sample_conv(mqk_s, conv_buf, conv_w[l], conv_b[l])
    c_t = jnp.transpose(state_C[l], (1, 2, 3, 0))
    n_t = jnp.transpose(state_n[l], (1, 2, 0))
    m_t = state_m[l].T
    c_s, n_s, m_s, mo_ts = _sample_mlstm(qkt_s, mvot_s[0:ML_COLS], mvot_s[ML_COLS:2 * ML_COLS], gates_ts, m_t, c_t,
                                         n_t, ml_norm_g[l])
    n_pool = cache_k.shape[1]
    ck_t = jnp.transpose(cache_k[l], (0, 2, 3, 1)).reshape(n_pool, c, PAGE_SIZE)
    cv_t = jnp.transpose(cache_v[l], (0, 2, 3, 1)).reshape(n_pool, c, PAGE_SIZE)
    attn_ts = _sample_attention(qkvt_s[0:c], qkvt_s[c:2 * c], qkvt_s[2 * c:3 * c], ck_t, cv_t,
                                page_table.T, rel_bias, da_norm_g[l], lams, lam_init)

    xs1_p, xs1_s, h2t, logits_t = _outproj(attn_p, mo_p, xp, modp[2:6], attn_ts, mo_ts, xs, mods[2:6], w_out_b, n2,
                                           w_router_t, wsg, wsu, wsd)

    n_e = w_router.shape[2]
    eidx, wts, rank, cnt = _route(logits_t, router_bias[l])
    counts = cnt[:, 0].astype(I32)
    padded = (counts + MOE_BM - 1) // MOE_BM * MOE_BM
    pad_end = jnp.cumsum(padded)
    offs = pad_end - padded
    n_blocks = -(-(t_all * TOP_K) // MOE_BM) + n_e
    block_e = jnp.minimum(jnp.searchsorted(pad_end, jnp.arange(n_blocks) * MOE_BM, side='right'),
                          n_e - 1).astype(I32)
    n_valid = (pad_end[-1:] // MOE_BM).astype(I32)
    pos = _positions(eidx, rank, offs)
    xs_sorted = _dispatch(pos, h2t, n_blocks * MOE_BM)
    ys = _experts(block_e, n_valid, xs_sorted, w_exp_gate[l], w_exp_up[l], w_exp_down[l], n_blocks)
    w_tm = wts.T
    y_p, y_s = _combine(pos, w_tm, xs1_p, modp[5], xs1_s, mods[5], final_norm_g, ys)

    def per_head(a_t, n_feat):
        return jnp.transpose(a_t.reshape(-1, n_feat, a_t.shape[1]), (2, 0, 1))

    return (
        y_p.reshape(bp, t, d),
        y_s.reshape(bs, ts, d),
        per_head(ktf, 2 * DA_QK_DIM).reshape(1, bp, t, DA_HEADS, 2 * DA_QK_DIM),
        per_head(vtf, DA_V_DIM).reshape(1, bp, t, DA_HEADS, DA_V_DIM),
        s_p[:, :, :hd].reshape(1, bp, ML_HEADS, hd, hd),
        s_p[:, :, hd].reshape(1, bp, ML_HEADS, hd),
        m_p.reshape(1, bp, ML_HEADS),
        mqk[t - (CONV_W - 1):, :].reshape(1, bp, CONV_W - 1, 2 * ML_COLS),
        per_head(qkvt_s[c:2 * c], 2 * DA_QK_DIM).reshape(1, bs, ts, DA_HEADS, 2 * DA_QK_DIM),
        per_head(qkvt_s[2 * c:3 * c], DA_V_DIM).reshape(1, bs, ts, DA_HEADS, DA_V_DIM),
        jnp.transpose(c_s, (3, 0, 1, 2)).reshape(1, bs, ML_HEADS, hd, hd),
        jnp.transpose(n_s, (2, 0, 1)).reshape(1, bs, ML_HEADS, hd),
        m_s.reshape(ML_HEADS, bs).T.reshape(1, bs, ML_HEADS),
        jnp.transpose(conv_new_s, (1, 0, 2)).reshape(1, bs, CONV_W - 1, 2 * ML_COLS),
    )
```
